```python
import jax
import jax.numpy as jnp
from jax import lax
import numpy as np

D_MODEL = 1024
BATCH = 16
SEQ = 2048
DEPTH = 4

GRID_W = 64
CTX_LEN = 256
N_MIXERS = 3

CONV_WIDTH = 31

NA_HEADS = 16
NA_HEAD_DIM = D_MODEL // NA_HEADS
NA_WIN_ROWS = 8
NA_WIN_COLS = 16
NA_QBLOCK_COLS = 16
NA_KBLOCK_COLS = NA_QBLOCK_COLS + NA_WIN_COLS

RWKV_HEAD_DIM = 64
RWKV_HEADS = D_MODEL // RWKV_HEAD_DIM
DECAY_LORA = 64
ICLR_LORA = 64
GATE_LORA = 128
N_LERP = 6
N_DIR = 2

N_EXPERTS = 16
EC_CAPACITY = 2
D_EXPERT = 2 * D_MODEL

DN_ALPHA = (2 * DEPTH) ** 0.25
DN_BETA = (8 * DEPTH) ** -0.25
LN_EPS = 1e-5
GN_EPS = 64e-5

N_CONV_LAYERS = (DEPTH + 2) // 3
N_NA_LAYERS = (DEPTH + 1) // 3
N_RWKV_LAYERS = DEPTH // 3

kernel_name = 'hybrid_conv_natten_rwkv7_ecmoe_diffusion'


def _layer_norm(x, g, b, eps=LN_EPS):
    xf = x.astype(jnp.float32)
    mu = jnp.mean(xf, -1, keepdims=True)
    var = jnp.mean(jnp.square(xf - mu), -1, keepdims=True)
    return ((xf - mu) * lax.rsqrt(var + eps)).astype(x.dtype) * g + b


def _ada(cond, w, b):
    m = jax.nn.silu(cond) @ w + b
    return jnp.split(m[..., None, :], 6, axis=-1)


def _modulate(h, shift, scale):
    return h * (1 + scale) + shift


def _post_norm(h, y, gate, g, b):
    return _layer_norm(DN_ALPHA * h + gate * y, g, b)


def _depthwise_conv(u, w, b):
    width = w.shape[0]
    y = lax.conv_general_dilated(u, w[:, None, :], window_strides=(1,),
                                 padding=[(width // 2, width // 2)],
                                 dimension_numbers=('NWC', 'WIO', 'NWC'),
                                 feature_group_count=u.shape[-1])
    return y + b


def _conformer_conv(h, w_in, b_in, w_dw, b_dw, n_g, n_b, w_out, b_out):
    u = h @ w_in + b_in
    val, gt = jnp.split(u, 2, axis=-1)
    u = val * jax.nn.sigmoid(gt)
    u = _depthwise_conv(u, w_dw, b_dw)
    u = jax.nn.silu(_layer_norm(u, n_g, n_b))
    return u @ w_out + b_out


def _na_latent(q, k, v, k_ctx, v_ctx, rpb):
    B, N, H, Dh = q.shape
    rows = N // GRID_W
    kh = min(NA_WIN_ROWS, rows)
    ncb = GRID_W // NA_QBLOCK_COLS
    qcol = np.arange(GRID_W).reshape(ncb, NA_QBLOCK_COLS)
    win_c0 = np.clip(qcol - NA_WIN_COLS // 2, 0, GRID_W - NA_WIN_COLS)
    blk_c0 = np.clip(np.arange(ncb) * NA_QBLOCK_COLS - NA_WIN_COLS // 2, 0, GRID_W - NA_KBLOCK_COLS)
    kcol = blk_c0[:, None] + np.arange(NA_KBLOCK_COLS)
    col_ok = (kcol[:, None, :] >= win_c0[:, :, None]) & (kcol[:, None, :] < win_c0[:, :, None] + NA_WIN_COLS)
    cidx = np.clip(kcol[:, None, :] - qcol[:, :, None] + NA_WIN_COLS - 1, 0, 2 * NA_WIN_COLS - 2)
    row0 = np.clip(np.arange(rows) - kh // 2, 0, rows - kh)
    ridx = row0[:, None] + np.arange(kh)[None, :] - np.arange(rows)[:, None] + NA_WIN_ROWS - 1
    kg = k.reshape(B, rows, GRID_W, H, Dh)
    vg = v.reshape(B, rows, GRID_W, H, Dh)
    qg = jnp.moveaxis(q.reshape(B, rows, ncb, NA_QBLOCK_COLS, H, Dh), 1, 0) * (Dh ** -0.5)
    mask = jnp.asarray(col_ok)[None, None, :, :, None, :]
    n_loc = kh * NA_KBLOCK_COLS

    def row_block(args):
        q_r, r0, ridx_r = args
        k_band = lax.dynamic_slice_in_dim(kg, r0, kh, axis=1)[:, :, kcol]
        v_band = lax.dynamic_slice_in_dim(vg, r0, kh, axis=1)[:, :, kcol]
        bias = jnp.transpose(rpb[:, ridx_r][:, :, cidx], (0, 2, 3, 1, 4)).astype(jnp.float32)
        s_loc = jnp.einsum('bjqhd,bajmhd->bhjqam', q_r, k_band).astype(jnp.float32) + bias
        s_loc = jnp.where(mask, s_loc, -jnp.inf)
        s_ctx = jnp.einsum('bjqhd,blhd->bhjql', q_r, k_ctx).astype(jnp.float32)
        s = jnp.concatenate([s_loc.reshape(s_loc.shape[:4] + (n_loc,)), s_ctx], axis=-1)
        p = jax.nn.softmax(s, axis=-1).astype(v.dtype)
        p_loc = p[..., :n_loc].reshape(s_loc.shape)
        return (jnp.einsum('bhjqam,bajmhd->bjqhd', p_loc, v_band)
                + jnp.einsum('bhjql,blhd->bjqhd', p[..., n_loc:], v_ctx))

    o = lax.map(row_block, (qg, jnp.asarray(row0, jnp.int32), jnp.asarray(ridx, jnp.int32)))
    return jnp.moveaxis(o, 0, 1).reshape(B, N, H * Dh)


def _na_mixer(u, uc, w_qkv, w_o, rpb, ctx_out):
    B, N, D = u.shape
    L = uc.shape[1]
    H, Dh = NA_HEADS, NA_HEAD_DIM
    qkv = (u @ w_qkv).reshape(B, N, 3, H, Dh)
    if ctx_out:
        qkv_c = (uc @ w_qkv).reshape(B, L, 3, H, Dh)
        k_c, v_c = qkv_c[:, :, 1], qkv_c[:, :, 2]
    else:
        kv_c = (uc @ w_qkv[:, D:]).reshape(B, L, 2, H, Dh)
        k_c, v_c = kv_c[:, :, 0], kv_c[:, :, 1]
    y = _na_latent(qkv[:, :, 0], qkv[:, :, 1], qkv[:, :, 2], k_c, v_c, rpb) @ w_o
    yc = None
    if ctx_out:
        s = jnp.einsum('blhd,bmhd->bhlm', qkv_c[:, :, 0] * (Dh ** -0.5), k_c).astype(jnp.float32)
        p = jax.nn.softmax(s, axis=-1).astype(v_c.dtype)
        yc = jnp.einsum('bhlm,bmhd->blhd', p, v_c).reshape(B, L, D) @ w_o
    return y, yc


def _shift_prev(h):
    return jnp.pad(h, ((0, 0), (1, 0), (0, 0)))[:, :-1]


def _shift_next(h):
    return jnp.pad(h, ((0, 0), (0, 1), (0, 0)))[:, 1:]


def _wkv7_scan(s0, w, k, v, a, b, r, reverse):
    seq = [jnp.swapaxes(t, 0, 1).astype(jnp.float32) for t in (w, k, v, a, b)]
    emit = r is not None
    if emit:
        seq.append(jnp.swapaxes(r, 0, 1).astype(jnp.float32))

    def step(S, inp):
        w_t, k_t, v_t, a_t, b_t = inp[:5]
        sa = jnp.einsum('bhvk,bhk->bhv', S, a_t)
        S = S * w_t[:, :, None, :] + sa[..., None] * b_t[:, :, None, :] + v_t[..., None] * k_t[:, :, None, :]
        return S, (jnp.einsum('bhvk,bhk->bhv', S, inp[5]) if emit else None)

    s_fin, ys = lax.scan(step, s0, tuple(seq), reverse=reverse)
    return s_fin, (jnp.swapaxes(ys, 0, 1) if emit else None)


def _head_norm(y, g, b):
    H, Dh = y.shape[-2:]
    mu = jnp.mean(y, -1, keepdims=True)
    var = jnp.mean(jnp.square(y - mu), -1, keepdims=True)
    return (y - mu) * lax.rsqrt(var + GN_EPS) * g.reshape(H, Dh) + b.reshape(H, Dh)


def _rwkv7_mixer(h, s0, emit, mu_prev, mu_next, w_r, w_k, w_v, w0, w1, w2, a0, a1, a2,
                 k_k, k_a, r_k, g1, g2, gn_g, gn_b, w_o):
    B, T, D = h.shape
    H, Dh = RWKV_HEADS, RWKV_HEAD_DIM
    d_prev = _shift_prev(h) - h
    d_next = _shift_next(h) - h

    def lerp(n):
        return h + d_prev * mu_prev[n] + d_next * mu_next[n]

    def heads(t):
        return t.reshape(B, T, H, Dh)

    xw, xa = lerp(1), lerp(4)
    k = heads(lerp(2) @ w_k)
    v = heads(lerp(3) @ w_v)
    kk = (k * k_k.reshape(H, Dh)).astype(jnp.float32)
    kk = kk * lax.rsqrt(jnp.maximum(jnp.sum(kk * kk, -1, keepdims=True), 1e-24))
    r = heads(lerp(0) @ w_r) if emit else None
    states, reads = [], []
    for d in range(N_DIR):
        w_log = -jax.nn.softplus(-(w0[d] + jnp.tanh(xw @ w1[d]) @ w2[d]).astype(jnp.float32)) - 0.5
        decay = heads(jnp.exp(-jnp.exp(w_log)))
        a = heads(jax.nn.sigmoid(a0[d] + (xa @ a1[d]) @ a2[d]))
        k_d = k * (1 + (a - 1) * k_a.reshape(H, Dh))
        init = jnp.zeros((B, H, Dh, Dh), jnp.float32) if s0 is None else s0[d]
        s_fin, y = _wkv7_scan(init, decay, k_d, v, -kk, kk * a, r, reverse=(d == 1))
        states.append(s_fin)
        if emit:
            bonus = jnp.sum(r * k_d * r_k, -1, keepdims=True) * v
            reads.append(_head_norm(y, gn_g, gn_b).astype(h.dtype) + bonus)
    if not emit:
        return None, states
    g = jax.nn.sigmoid(lerp(5) @ g1) @ g2
    return ((reads[0] + reads[1]).reshape(B, T, D) * g) @ w_o, states


def _ec_moe(h, w_router, w_gate, w_up, w_down):
    B, T, D = h.shape
    cap = max(1, EC_CAPACITY * T // N_EXPERTS)
    aff = jax.nn.softmax((h @ w_router).astype(jnp.float32), axis=-1)
    gates, idx = lax.top_k(jnp.swapaxes(aff, 1, 2), cap)
    xs = jax.vmap(lambda hb, ib: hb[ib])(h, idx)
    hid = jax.nn.silu(jnp.einsum('becd,edf->becf', xs, w_gate)) * jnp.einsum('becd,edf->becf', xs, w_up)
    y = jnp.einsum('becf,efd->becd', hid, w_down) * gates[..., None].astype(h.dtype)
    return jax.vmap(lambda yb, ib: jnp.zeros((T, D), yb.dtype).at[ib.reshape(-1)].add(yb.reshape(-1, D)))(y, idx)


def setup_inputs(seed: int = 0) -> dict:
    key = jax.random.key(seed)
    ks = iter(jax.random.split(key, 48))
    D = D_MODEL
    f = D ** -0.5

    def nrm(shape, s):
        return s * jax.random.normal(next(ks), shape, jnp.float32)

    def uni(shape, lo, hi):
        return jax.random.uniform(next(ks), shape, jnp.float32, lo, hi)

    NC, NN, NR = N_CONV_LAYERS, N_NA_LAYERS, N_RWKV_LAYERS
    return {
        'x': nrm((BATCH, SEQ, D), 1.0),
        'c': nrm((BATCH, D), 1.0),
        'ctx': nrm((BATCH, CTX_LEN, D), 1.0),
        'c_ctx': nrm((D,), 1.0),
        'ada_w': nrm((DEPTH, D, 6 * D), 0.5 * f),
        'ada_b': nrm((DEPTH, 6 * D), 0.02),
        'ln_g': 1.0 + nrm((DEPTH, 2, D), 0.02),
        'ln_b': nrm((DEPTH, 2, D), 0.02),
        'conv_w_in': nrm((NC, D, 2 * D), f),
        'conv_b_in': nrm((NC, 2 * D), 0.02),
        'conv_w_dw': nrm((NC, CONV_WIDTH, D), CONV_WIDTH ** -0.5),
        'conv_b_dw': nrm((NC, D), 0.02),
        'conv_ln_g': 1.0 + nrm((NC, D), 0.02),
        'conv_ln_b': nrm((NC, D), 0.02),
        'conv_w_out': nrm((NC, D, D), f * DN_BETA),
        'conv_b_out': nrm((NC, D), 0.02),
        'na_w_qkv': nrm((NN, D, 3 * D), f),
        'na_w_o': nrm((NN, D, D), f * DN_BETA),
        'na_rpb': nrm((NN, NA_HEADS, 2 * NA_WIN_ROWS - 1, 2 * NA_WIN_COLS - 1), 0.05),
        'rw_mu_prev': uni((NR, N_LERP, D), 0.0, 1.0),
        'rw_mu_next': uni((NR, N_LERP, D), 0.0, 1.0),
        'rw_w_r': nrm((NR, D, D), f),
        'rw_w_k': nrm((NR, D, D), f),
        'rw_w_v': nrm((NR, D, D), f),
        'rw_w0': uni((NR, N_DIR, D), -4.0, 1.0),
        'rw_w1': nrm((NR, N_DIR, D, DECAY_LORA), f),
        'rw_w2': nrm((NR, N_DIR, DECAY_LORA, D), 0.1 * DECAY_LORA ** -0.5),
        'rw_a0': nrm((NR, N_DIR, D), 0.1),
        'rw_a1': nrm((NR, N_DIR, D, ICLR_LORA), f),
        'rw_a2': nrm((NR, N_DIR, ICLR_LORA, D), 0.1 * ICLR_LORA ** -0.5),
        'rw_k_k': 0.85 + nrm((NR, D), 0.05),
        'rw_k_a': 1.0 + nrm((NR, D), 0.05),
        'rw_r_k': nrm((NR, RWKV_HEADS, RWKV_HEAD_DIM), 0.1),
        'rw_g1': nrm((NR, D, GATE_LORA), f),
        'rw_g2': nrm((NR, GATE_LORA, D), GATE_LORA ** -0.5),
        'rw_gn_g': 1.0 + nrm((NR, D), 0.02),
        'rw_gn_b': nrm((NR, D), 0.02),
        'rw_w_o': nrm((NR, D, D), f * DN_BETA),
        'moe_router': nrm((DEPTH, D, N_EXPERTS), f),
        'moe_w_gate': nrm((DEPTH, N_EXPERTS, D, D_EXPERT), f),
        'moe_w_up': nrm((DEPTH, N_EXPERTS, D, D_EXPERT), f),
        'moe_w_down': nrm((DEPTH, N_EXPERTS, D_EXPERT, D), D_EXPERT ** -0.5 * DN_BETA),
    }


def reference(x, c, ctx, c_ctx, ada_w, ada_b, ln_g, ln_b,
              conv_w_in, conv_b_in, conv_w_dw, conv_b_dw, conv_ln_g, conv_ln_b, conv_w_out, conv_b_out,
              na_w_qkv, na_w_o, na_rpb,
              rw_mu_prev, rw_mu_next, rw_w_r, rw_w_k, rw_w_v, rw_w0, rw_w1, rw_w2, rw_a0, rw_a1, rw_a2,
              rw_k_k, rw_k_a, rw_r_k, rw_g1, rw_g2, rw_gn_g, rw_gn_b, rw_w_o,
              moe_router, moe_w_gate, moe_w_up, moe_w_down):
    ctx_layers = [i for i in range(DEPTH) if i % N_MIXERS != 0]
    last_ctx = max(ctx_layers) if ctx_layers else -1
    h, hc = x, ctx
    for i in range(DEPTH):
        kind, j = i % N_MIXERS, i // N_MIXERS
        ctx_read = i <= last_ctx
        ctx_live = i < last_ctx
        sh1, sc1, ga1, sh2, sc2, ga2 = _ada(c, ada_w[i], ada_b[i])
        u = _modulate(h, sh1, sc1)
        uc = None
        if ctx_read:
            csh1, csc1, cga1, csh2, csc2, cga2 = _ada(c_ctx, ada_w[i], ada_b[i])
            uc = _modulate(hc, csh1, csc1)
        if kind == 0:
            conv_p = (conv_w_in[j], conv_b_in[j], conv_w_dw[j], conv_b_dw[j],
                      conv_ln_g[j], conv_ln_b[j], conv_w_out[j], conv_b_out[j])
            y = _conformer_conv(u, *conv_p)
            yc = _conformer_conv(uc, *conv_p) if ctx_live else None
        elif kind == 1:
            y, yc = _na_mixer(u, uc, na_w_qkv[j], na_w_o[j], na_rpb[j], ctx_live)
        else:
            rw_p = (rw_mu_prev[j], rw_mu_next[j], rw_w_r[j], rw_w_k[j], rw_w_v[j],
                    rw_w0[j], rw_w1[j], rw_w2[j], rw_a0[j], rw_a1[j], rw_a2[j],
                    rw_k_k[j], rw_k_a[j], rw_r_k[j], rw_g1[j], rw_g2[j],
                    rw_gn_g[j], rw_gn_b[j], rw_w_o[j])
            yc, ctx_states = _rwkv7_mixer(uc, None, ctx_live, *rw_p)
            y, _ = _rwkv7_mixer(u, ctx_states, True, *rw_p)
        moe_p = (moe_router[i], moe_w_gate[i], moe_w_up[i], moe_w_down[i])
        h = _post_norm(h, y, ga1, ln_g[i, 0], ln_b[i, 0])
        h = _post_norm(h, _ec_moe(_modulate(h, sh2, sc2), *moe_p), ga2, ln_g[i, 1], ln_b[i, 1])
        if ctx_live:
            hc = _post_norm(hc, yc, cga1, ln_g[i, 0], ln_b[i, 0])
            hc = _post_norm(hc, _ec_moe(_modulate(hc, csh2, csc2), *moe_p), cga2, ln_g[i, 1], ln_b[i, 1])
    return h
```

```python
import functools

import numpy as np
import jax
import jax.numpy as jnp
from jax import lax
from jax.experimental import pallas as pl
from jax.experimental.pallas import tpu as pltpu

DEPTH = 4
N_MIXERS = 3
GRID_W = 64
CONV_WIDTH = 31
NA_HEADS = 16
NA_WIN_ROWS = 8
NA_WIN_COLS = 16
RWKV_HEAD_DIM = 64
N_EXPERTS = 16
EC_CAPACITY = 2
DN_ALPHA = (2 * DEPTH) ** 0.25
LN_EPS = 1e-5
GN_EPS = 64e-5

LANES = 128
SUBLANES = 8
VMEM_LIMIT = 56 * 1024 * 1024

NEG_BIG = -1e30

_f32 = jnp.float32
_bf16 = jnp.bfloat16


def _cparams(sem):
    return pltpu.CompilerParams(dimension_semantics=sem, vmem_limit_bytes=VMEM_LIMIT)


def _sigmoid(x):
    return 1.0 / (1.0 + jnp.exp(-x))


def _silu(x):
    return x * _sigmoid(x)


def _ln_rows(x, g, b, eps):
    mu = jnp.mean(x, axis=-1, keepdims=True)
    xc = x - mu
    var = jnp.mean(xc * xc, axis=-1, keepdims=True)
    return xc * lax.rsqrt(var + eps) * g + b


def _dot(a, b):
    return jnp.dot(a, b, preferred_element_type=_f32)


def _dot_nt(a, b):
    return lax.dot_general(a, b, (((1,), (1,)), ((), ())), preferred_element_type=_f32)


def _ada_kernel(c_ref, w_ref, b_ref, o_ref):
    s = _silu(c_ref[...]).astype(_bf16)
    o_ref[0] = _dot(s, w_ref[0]) + b_ref[0]


def _ada_all(cond, ada_w, ada_b):
    R, D = cond.shape
    L, _, N = ada_w.shape
    tn = D
    return pl.pallas_call(
        _ada_kernel,
        grid=(L, N // tn),
        in_specs=[pl.BlockSpec((R, D), lambda l, n: (0, 0)),
                  pl.BlockSpec((1, D, tn), lambda l, n: (l, 0, n)),
                  pl.BlockSpec((1, 1, tn), lambda l, n: (l, 0, n))],
        out_specs=pl.BlockSpec((1, R, tn), lambda l, n: (l, 0, n)),
        out_shape=jax.ShapeDtypeStruct((L, R, N), _f32),
        compiler_params=_cparams(("arbitrary", "arbitrary")),
        name="ada",
    )(cond, ada_w, ada_b)


def _mm_kernel(*refs, pro, epi, has_bias, tm):
    it = iter(refs)
    x_ref = next(it)
    x = x_ref[0]
    if pro in ("mod", "mod_lerp"):
        sh_ref, sc_ref = next(it), next(it)
        sh, sc1 = sh_ref[0], 1.0 + sc_ref[0]
        x = x * sc1 + sh
    if pro == "mod_lerp":
        hp_ref, hn_ref, mup_ref, mun_ref = next(it), next(it), next(it), next(it)
        t = pl.program_id(2)
        nt = pl.num_programs(2)
        prev_row = jnp.where(t == 0, 0.0, hp_ref[0, 0] * sc1 + sh)
        next_row = jnp.where(t == nt - 1, 0.0, hn_ref[0, 0] * sc1 + sh)
        rows = lax.broadcasted_iota(jnp.int32, x.shape, 0)
        x_prev = jnp.where(rows == 0, prev_row, pltpu.roll(x, 1, 0))
        x_next = jnp.where(rows == tm - 1, next_row, pltpu.roll(x, tm - 1, 0))
        x = x + (x_prev - x) * mup_ref[...] + (x_next - x) * mun_ref[...]
    if pro == "ln_silu":
        g_ref, b_ref = next(it), next(it)
        x = _silu(_ln_rows(x, g_ref[...], b_ref[...], LN_EPS))
    if pro == "sum_gate":
        x2_ref, xg_ref = next(it), next(it)
        x = (x + x2_ref[0]) * xg_ref[0]
    xb = x.astype(_bf16)
    w_ref = next(it)
    y = _dot(xb, w_ref[...])
    if epi == "glu":
        w2_ref = next(it)
        y2 = _dot(xb, w2_ref[...])
    if has_bias:
        y = y + next(it)[...]
        if epi == "glu":
            y2 = y2 + next(it)[...]
    if epi == "glu":
        y = y * _sigmoid(y2)
    if epi == "postnorm":
        h_ref, ga_ref, g_ref, b_ref = next(it), next(it), next(it), next(it)
        y = _ln_rows(DN_ALPHA * h_ref[0] + ga_ref[0] * y, g_ref[...], b_ref[...], LN_EPS)
    o_ref = next(it)
    o_ref[0] = y.astype(o_ref.dtype)


def _row_tile(T):
    return 512 if T % 512 == 0 else T


def _mm(x, w, *, pro=None, epi=None, bias=None, mod=None, lerp=None, ln=None, extra=None, post=None,
        tn=None, name="mm"):
    B, T, K = x.shape
    N = w.shape[1]
    n_out = N // 2 if epi == "glu" else N
    tm = _row_tile(T)
    tn = n_out if tn is None else tn
    nt, nn = T // tm, n_out // tn
    grid = (nn, B, nt)
    xmap = lambda n, b, t: (b, t, 0)
    vecmap = lambda n, b, t: (b, 0, 0)
    args, specs = [x], [pl.BlockSpec((1, tm, K), xmap)]
    if pro in ("mod", "mod_lerp"):
        shift, scale = mod
        args += [shift, scale]
        specs += [pl.BlockSpec((1, 1, K), vecmap)] * 2
    if pro == "mod_lerp":
        mu_prev, mu_next = lerp
        zero = jnp.zeros((B, 1, K), x.dtype)
        hp = jnp.concatenate([zero, x[:, tm - 1:T - 1:tm]], axis=1)[:, :, None]
        hn = jnp.concatenate([x[:, tm::tm], zero], axis=1)[:, :, None]
        args += [hp, hn, mu_prev, mu_next]
        specs += [pl.BlockSpec((1, 1, 1, K), lambda n, b, t: (b, t, 0, 0))] * 2
        specs += [pl.BlockSpec((1, K), lambda n, b, t: (0, 0))] * 2
    if pro == "ln_silu":
        args += list(ln)
        specs += [pl.BlockSpec((1, K), lambda n, b, t: (0, 0))] * 2
    if pro == "sum_gate":
        args += list(extra)
        specs += [pl.BlockSpec((1, tm, K), xmap)] * 2
    args.append(w)
    specs.append(pl.BlockSpec((K, tn), lambda n, b, t: (0, n)))
    if epi == "glu":
        args.append(w)
        specs.append(pl.BlockSpec((K, tn), lambda n, b, t: (0, nn + n)))
    if bias is not None:
        args.append(bias)
        specs.append(pl.BlockSpec((1, tn), lambda n, b, t: (0, n)))
        if epi == "glu":
            args.append(bias)
            specs.append(pl.BlockSpec((1, tn), lambda n, b, t: (0, nn + n)))
    if epi == "postnorm":
        assert tn == n_out
        h, gate, g, b = post
        args += [h, gate, g, b]
        specs += [pl.BlockSpec((1, tm, n_out), xmap), pl.BlockSpec((1, 1, n_out), vecmap),
                  pl.BlockSpec((1, n_out), lambda n, b_, t: (0, 0)), pl.BlockSpec((1, n_out), lambda n, b_, t: (0, 0))]
    kern = functools.partial(_mm_kernel, pro=pro, epi=epi, has_bias=bias is not None, tm=tm)
    return pl.pallas_call(
        kern, grid=grid, in_specs=specs,
        out_specs=pl.BlockSpec((1, tm, tn), lambda n, b, t: (b, t, n)),
        out_shape=jax.ShapeDtypeStruct((B, T, n_out), _f32),
        compiler_params=_cparams(("arbitrary", "arbitrary", "arbitrary")),
        name=name,
    )(*args)


CONV_PAD = 16
CONV_ROWS = 64


def _dwconv_kernel(x_ref, w_ref, b_ref, o_ref, pad_ref, *, T):
    half = CONV_WIDTH // 2
    zeros = jnp.zeros((CONV_PAD, LANES), _f32)
    pad_ref[pl.ds(0, CONV_PAD), :] = zeros
    pad_ref[pl.ds(CONV_PAD + T, CONV_PAD), :] = zeros
    pad_ref[pl.ds(CONV_PAD, T), :] = x_ref[0]
    w = w_ref[...]
    bias = b_ref[...]

    def chunk(i, carry):
        r0 = pl.multiple_of(i * CONV_ROWS, CONV_ROWS)
        win = pad_ref.at[pl.ds(r0, CONV_ROWS + 2 * CONV_PAD), :]
        acc = jnp.zeros((CONV_ROWS, LANES), _f32) + bias
        for j in range(CONV_WIDTH):
            acc = acc + win[pl.ds(CONV_PAD - half + j, CONV_ROWS), :] * w[j:j + 1, :]
        o_ref[0, pl.ds(r0, CONV_ROWS), :] = acc
        return carry

    lax.fori_loop(0, T // CONV_ROWS, chunk, 0)


def _dwconv(u, w_dw, b_dw):
    B, T, C = u.shape
    return pl.pallas_call(
        functools.partial(_dwconv_kernel, T=T),
        grid=(B, C // LANES),
        in_specs=[pl.BlockSpec((1, T, LANES), lambda b, c: (b, 0, c)),
                  pl.BlockSpec((CONV_WIDTH, LANES), lambda b, c: (0, c)),
                  pl.BlockSpec((1, LANES), lambda b, c: (0, c))],
        out_specs=pl.BlockSpec((1, T, LANES), lambda b, c: (b, 0, c)),
        out_shape=jax.ShapeDtypeStruct((B, T, C), _f32),
        scratch_shapes=[pltpu.VMEM((T + 2 * CONV_PAD, LANES), _f32)],
        compiler_params=_cparams(("arbitrary", "arbitrary")),
        name="dwconv",
    )(u, w_dw, b_dw)


NA_Q_ROWS = 4
NA_BAND_ROWS = NA_Q_ROWS + NA_WIN_ROWS


def _na_bias_tables(rpb, rows):
    out = []
    for r_s in (0, NA_Q_ROWS, rows - NA_Q_ROWS):
        start = int(np.clip(r_s - NA_WIN_ROWS // 2, 0, rows - NA_BAND_ROWS))
        q_row = r_s + np.repeat(np.arange(NA_Q_ROWS), GRID_W)
        q_col = np.tile(np.arange(GRID_W), NA_Q_ROWS)
        k_row = start + np.repeat(np.arange(NA_BAND_ROWS), GRID_W)
        k_col = np.tile(np.arange(GRID_W), NA_BAND_ROWS)
        row0 = np.clip(q_row - NA_WIN_ROWS // 2, 0, rows - NA_WIN_ROWS)
        col0 = np.clip(q_col - NA_WIN_COLS // 2, 0, GRID_W - NA_WIN_COLS)
        ok = ((k_row[None, :] >= row0[:, None]) & (k_row[None, :] < row0[:, None] + NA_WIN_ROWS)
              & (k_col[None, :] >= col0[:, None]) & (k_col[None, :] < col0[:, None] + NA_WIN_COLS))
        ridx = np.clip(k_row[None, :] - q_row[:, None] + NA_WIN_ROWS - 1, 0, 2 * NA_WIN_ROWS - 2)
        cidx = np.clip(k_col[None, :] - q_col[:, None] + NA_WIN_COLS - 1, 0, 2 * NA_WIN_COLS - 2)
        out.append(jnp.where(jnp.asarray(ok)[None], rpb[:, ridx, cidx], NEG_BIG))
    return jnp.stack(out)


def _na_kernel(q_ref, k_ref, v_ref, kc_ref, vc_ref, bias_ref, o_ref, *, rows, dh):
    rt = pl.program_id(2)
    n_rt = pl.num_programs(2)
    nq = NA_Q_ROWS * GRID_W
    nk = NA_BAND_ROWS * GRID_W
    start = jnp.clip(rt * NA_Q_ROWS - NA_WIN_ROWS // 2, 0, rows - NA_BAND_ROWS) * GRID_W
    start = pl.multiple_of(start, GRID_W)
    variant = jnp.where(rt == 0, 0, jnp.where(rt == n_rt - 1, 2, 1))
    q = q_ref[0] * (dh ** -0.5)
    k = k_ref[0, pl.ds(start, nk), :].astype(_bf16)
    v = v_ref[0, pl.ds(start, nk), :].astype(_bf16)
    kc = kc_ref[0].astype(_bf16)
    vc = vc_ref[0].astype(_bf16)
    lane = lax.broadcasted_iota(jnp.int32, (nq, LANES), 1)
    out = jnp.zeros((nq, LANES), _f32)
    for h in range(LANES // dh):
        in_head = (lane >= h * dh) & (lane < (h + 1) * dh)
        qh = jnp.where(in_head, q, 0.0).astype(_bf16)
        s_loc = _dot_nt(qh, k) + bias_ref[variant, 0, h]
        s_ctx = _dot_nt(qh, kc)
        m = jnp.maximum(jnp.max(s_loc, axis=-1, keepdims=True), jnp.max(s_ctx, axis=-1, keepdims=True))
        p_loc = jnp.exp(s_loc - m)
        p_ctx = jnp.exp(s_ctx - m)
        denom = jnp.sum(p_loc, axis=-1, keepdims=True) + jnp.sum(p_ctx, axis=-1, keepdims=True)
        o = (_dot(p_loc.astype(_bf16), v) + _dot(p_ctx.astype(_bf16), vc)) / denom
        out = jnp.where(in_head, o, out)
    o_ref[0] = out


def _na_attention(qkv, qkv_c, bias_tab):
    B, T, D3 = qkv.shape
    D = D3 // 3
    L = qkv_c.shape[1]
    rows = T // GRID_W
    dh = D // NA_HEADS
    ncb = D // LANES
    nq = NA_Q_ROWS * GRID_W
    nk = NA_BAND_ROWS * GRID_W
    per = LANES // dh
    return pl.pallas_call(
        functools.partial(_na_kernel, rows=rows, dh=dh),
        grid=(ncb, B, rows // NA_Q_ROWS),
        in_specs=[pl.BlockSpec((1, nq, LANES), lambda c, b, r: (b, r, c)),
                  pl.BlockSpec((1, T, LANES), lambda c, b, r: (b, 0, ncb + c)),
                  pl.BlockSpec((1, T, LANES), lambda c, b, r: (b, 0, 2 * ncb + c)),
                  pl.BlockSpec((1, L, LANES), lambda c, b, r: (b, 0, ncb + c)),
                  pl.BlockSpec((1, L, LANES), lambda c, b, r: (b, 0, 2 * ncb + c)),
                  pl.BlockSpec((3, 1, per, nq, nk), lambda c, b, r: (0, c, 0, 0, 0))],
        out_specs=pl.BlockSpec((1, nq, LANES), lambda c, b, r: (b, r, c)),
        out_shape=jax.ShapeDtypeStruct((B, T, D), _f32),
        compiler_params=_cparams(("arbitrary", "arbitrary", "arbitrary")),
        name="na_attn",
    )(qkv, qkv, qkv, qkv_c, qkv_c, bias_tab)


def _ctx_attn_kernel(q_ref, k_ref, v_ref, o_ref, *, dh):
    L = q_ref.shape[1]
    q = q_ref[0] * (dh ** -0.5)
    k = k_ref[0].astype(_bf16)
    v = v_ref[0].astype(_bf16)
    lane = lax.broadcasted_iota(jnp.int32, (L, LANES), 1)
    out = jnp.zeros((L, LANES), _f32)
    for h in range(LANES // dh):
        in_head = (lane >= h * dh) & (lane < (h + 1) * dh)
        qh = jnp.where(in_head, q, 0.0).astype(_bf16)
        s = _dot_nt(qh, k)
        p = jnp.exp(s - jnp.max(s, axis=-1, keepdims=True))
        o = _dot(p.astype(_bf16), v) / jnp.sum(p, axis=-1, keepdims=True)
        out = jnp.where(in_head, o, out)
    o_ref[0] = out


def _ctx_attention(qkv_c):
    B, L, D3 = qkv_c.shape
    D = D3 // 3
    ncb = D // LANES
    dh = D // NA_HEADS
    return pl.pallas_call(
        functools.partial(_ctx_attn_kernel, dh=dh),
        grid=(B, ncb),
        in_specs=[pl.BlockSpec((1, L, LANES), lambda b, c: (b, 0, c)),
                  pl.BlockSpec((1, L, LANES), lambda b, c: (b, 0, ncb + c)),
                  pl.BlockSpec((1, L, LANES), lambda b, c: (b, 0, 2 * ncb + c))],
        out_specs=pl.BlockSpec((1, L, LANES), lambda b, c: (b, 0, c)),
        out_shape=jax.ShapeDtypeStruct((B, L, D), _f32),
        compiler_params=_cparams(("arbitrary", "arbitrary")),
        name="ctx_attn",
    )(qkv_c, qkv_c, qkv_c)


def _rw_lora_kernel(x_ref, sh_ref, sc_ref, hp_ref, hn_ref, mup_ref, mun_ref, w1_ref, a1_ref, g1_ref,
                    w2_ref, a2_ref, g2_ref, w0_ref, a0_ref, lw_ref, ia_ref, *maybe_g, tm, emit):
    sh, sc1 = sh_ref[0], 1.0 + sc_ref[0]
    x = x_ref[0] * sc1 + sh
    t = pl.program_id(1)
    nt = pl.num_programs(1)
    prev_row = jnp.where(t == 0, 0.0, hp_ref[0, 0] * sc1 + sh)
    next_row = jnp.where(t == nt - 1, 0.0, hn_ref[0, 0] * sc1 + sh)
    rows = lax.broadcasted_iota(jnp.int32, x.shape, 0)
    d_prev = jnp.where(rows == 0, prev_row, pltpu.roll(x, 1, 0)) - x
    d_next = jnp.where(rows == tm - 1, next_row, pltpu.roll(x, tm - 1, 0)) - x

    def lerp(n):
        return (x + d_prev * mup_ref[n] + d_next * mun_ref[n]).astype(_bf16)

    tw = jnp.tanh(_dot(lerp(0), w1_ref[...])).astype(_bf16)
    ta = _dot(lerp(1), a1_ref[...]).astype(_bf16)
    for d in range(2):
        z = w0_ref[d] + _dot(tw, w2_ref[d])
        softplus_neg = jnp.maximum(-z, 0.0) + jnp.log(1.0 + jnp.exp(-jnp.abs(z)))
        lw_ref[d, 0] = -jnp.exp(-softplus_neg - 0.5)
        ia_ref[d, 0] = _sigmoid(a0_ref[d] + _dot(ta, a2_ref[d]))
    if emit:
        g_ref = maybe_g[0]
        tg = _sigmoid(_dot(lerp(2), g1_ref[...])).astype(_bf16)
        g_ref[0] = _dot(tg, g2_ref[...])


def _rw_lora(h, mod, mu_prev3, mu_next3, w1c, a1c, g1, w2p, a2p, g2, w0, a0, emit):
    B, T, D = h.shape
    tm = _row_tile(T)
    nt = T // tm
    shift, scale = mod
    zero = jnp.zeros((B, 1, D), h.dtype)
    hp = jnp.concatenate([zero, h[:, tm - 1:T - 1:tm]], axis=1)[:, :, None]
    hn = jnp.concatenate([h[:, tm::tm], zero], axis=1)[:, :, None]
    R2 = w1c.shape[1]
    G = g1.shape[1]
    c2 = lambda b, t: (0, 0)
    c3 = lambda b, t: (0, 0, 0)
    out_shape = [jax.ShapeDtypeStruct((2, B, T, D), _f32), jax.ShapeDtypeStruct((2, B, T, D), _f32)]
    out_specs = [pl.BlockSpec((2, 1, tm, D), lambda b, t: (0, b, t, 0))] * 2
    if emit:
        out_shape.append(jax.ShapeDtypeStruct((B, T, D), _f32))
        out_specs.append(pl.BlockSpec((1, tm, D), lambda b, t: (b, t, 0)))
    return pl.pallas_call(
        functools.partial(_rw_lora_kernel, tm=tm, emit=emit),
        grid=(B, nt),
        in_specs=[pl.BlockSpec((1, tm, D), lambda b, t: (b, t, 0)),
                  pl.BlockSpec((1, 1, D), lambda b, t: (b, 0, 0)),
                  pl.BlockSpec((1, 1, D), lambda b, t: (b, 0, 0)),
                  pl.BlockSpec((1, 1, 1, D), lambda b, t: (b, t, 0, 0)),
                  pl.BlockSpec((1, 1, 1, D), lambda b, t: (b, t, 0, 0)),
                  pl.BlockSpec((3, 1, D), c3), pl.BlockSpec((3, 1, D), c3),
                  pl.BlockSpec((D, R2), c2), pl.BlockSpec((D, R2), c2), pl.BlockSpec((D, G), c2),
                  pl.BlockSpec((2, R2, D), c3), pl.BlockSpec((2, R2, D), c3), pl.BlockSpec((G, D), c2),
                  pl.BlockSpec((2, 1, D), c3), pl.BlockSpec((2, 1, D), c3)],
        out_specs=out_specs, out_shape=out_shape,
        compiler_params=_cparams(("arbitrary", "arbitrary")),
        name="rw_lora",
    )(h, shift, scale, hp, hn, mu_prev3, mu_next3, w1c, a1c, g1, w2p, a2p, g2, w0, a0)


RW_TC = 32


def _rw_scan_kernel(*refs, emit, has_init, dh):
    it = iter(refs)
    lw_ref, ia_ref, k_ref, v_ref = next(it), next(it), next(it), next(it)
    r_ref = next(it) if emit else None
    kk_ref, ka_ref = next(it), next(it)
    if emit:
        rk_ref, gg_ref, gb_ref = next(it), next(it), next(it)
    s0_ref = next(it) if has_init else None
    y_ref = next(it) if emit else None
    sfin_ref = next(it)
    s_ref, w_s, a_s, b_s, kd_s = next(it), next(it), next(it), next(it), next(it)

    d = pl.program_id(0)
    c = pl.program_id(2)
    nc = pl.num_programs(2)

    @pl.when(c == 0)
    def _():
        if has_init:
            s_ref[...] = s0_ref[0, 0]
        else:
            s_ref[...] = jnp.zeros(s_ref.shape, _f32)

    def step(i, carry):
        t = jnp.where(d == 0, i, RW_TC - 1 - i)
        k_t = k_ref[t]
        ia_t = ia_ref[0, t]
        v_t = v_ref[t]
        kkr = k_t * kk_ref[...]
        kk = kkr * lax.rsqrt(jnp.maximum(jnp.sum(kkr * kkr, axis=0, keepdims=True), 1e-24))
        kd = k_t * (1.0 + (ia_t - 1.0) * ka_ref[...])
        w_s[...] = jnp.exp(lw_ref[0, t])
        a_s[...] = -kk
        b_s[...] = kk * ia_t
        kd_s[...] = kd
        if emit:
            r_t = r_ref[t]
        for vi in range(dh):
            sv = s_ref[vi]
            sa = jnp.sum(sv * a_s[...], axis=0, keepdims=True)
            sv = sv * w_s[...] + sa * b_s[...] + v_t[vi:vi + 1, :] * kd_s[...]
            s_ref[vi] = sv
            if emit:
                y_ref[0, t, pl.ds(vi, 1), :] = jnp.sum(sv * r_t, axis=0, keepdims=True)
        if emit:
            y = y_ref[0, t]
            mu = jnp.mean(y, axis=0, keepdims=True)
            yc = y - mu
            var = jnp.mean(yc * yc, axis=0, keepdims=True)
            bonus = jnp.sum(r_t * kd * rk_ref[...], axis=0, keepdims=True) * v_t
            y_ref[0, t] = yc * lax.rsqrt(var + GN_EPS) * gg_ref[...] + gb_ref[...] + bonus
        return carry

    lax.fori_loop(0, RW_TC, step, 0)

    @pl.when(c == nc - 1)
    def _():
        sfin_ref[0, 0] = s_ref[...]


def _rw_scan(lwT, iaT, kT, vT, rT, par, s0, emit):
    _, T, dh, S = lwT.shape
    nsb = S // LANES
    nc = T // RW_TC

    def tmap(d, s, c):
        return (jnp.where(d == 0, c, nc - 1 - c), 0, s)

    def dtmap(d, s, c):
        return (d, jnp.where(d == 0, c, nc - 1 - c), 0, s)

    tile = pl.BlockSpec((RW_TC, dh, LANES), tmap)
    dtile = pl.BlockSpec((1, RW_TC, dh, LANES), dtmap)
    ptile = pl.BlockSpec((dh, LANES), lambda d, s, c: (0, s))
    stile = pl.BlockSpec((1, 1, dh, dh, LANES), lambda d, s, c: (d, s, 0, 0, 0))
    args, specs = [lwT, iaT, kT, vT], [dtile, dtile, tile, tile]
    if emit:
        args.append(rT)
        specs.append(tile)
    args += [par["k_k"], par["k_a"]]
    specs += [ptile, ptile]
    if emit:
        args += [par["r_k"], par["gn_g"], par["gn_b"]]
        specs += [ptile] * 3
    if s0 is not None:
        args.append(s0)
        specs.append(stile)
    out_shape, out_specs = [], []
    if emit:
        out_shape.append(jax.ShapeDtypeStruct((2, T, dh, S), _f32))
        out_specs.append(dtile)
    out_shape.append(jax.ShapeDtypeStruct((2, nsb, dh, dh, LANES), _f32))
    out_specs.append(stile)
    res = pl.pallas_call(
        functools.partial(_rw_scan_kernel, emit=emit, has_init=s0 is not None, dh=dh),
        grid=(2, nsb, nc),
        in_specs=specs, out_specs=out_specs, out_shape=out_shape,
        scratch_shapes=[pltpu.VMEM((dh, dh, LANES), _f32)] + [pltpu.VMEM((dh, LANES), _f32)] * 4,
        compiler_params=_cparams(("arbitrary", "arbitrary", "arbitrary")),
        name="rw_scan_emit" if emit else "rw_scan_state",
    )(*args)
    return (res[0], res[1]) if emit else (None, res[0])


def _excl_cumsum_lanes(m):
    E, T = m.shape
    r = lax.broadcasted_iota(jnp.int32, (LANES, LANES), 0)
    c = lax.broadcasted_iota(jnp.int32, (LANES, LANES), 1)
    tri = jnp.where(r < c, 1.0, 0.0).astype(_bf16)
    out, offset = [], jnp.zeros((E, 1), _f32)
    for j in range(T // LANES):
        blk = m[:, j * LANES:(j + 1) * LANES]
        out.append(_dot(blk.astype(_bf16), tri) + offset)
        offset = offset + jnp.sum(blk, axis=1, keepdims=True)
    return jnp.concatenate(out, axis=1)


def _route_kernel(x_ref, sh_ref, sc_ref, wr_ref, xm_ref, slot_ref, gate_ref, lt_ref, *, tm, cap, n_exp):
    t = pl.program_id(1)
    nt = pl.num_programs(1)
    xm = x_ref[0] * (1.0 + sc_ref[0]) + sh_ref[0]
    xm_ref[0] = xm.astype(_bf16)
    logits = jnp.dot(xm, wr_ref[...], preferred_element_type=_f32, precision=lax.Precision.HIGHEST)
    lt_ref[:, pl.ds(pl.multiple_of(t * tm, tm), tm)] = logits.T

    @pl.when(t == nt - 1)
    def _():
        lg = lt_ref[0:n_exp, :]
        e = jnp.exp(lg - jnp.max(lg, axis=0, keepdims=True))
        aff = e / jnp.sum(e, axis=0, keepdims=True)
        bits = pltpu.bitcast(aff, jnp.int32)
        thr = jnp.zeros((n_exp, 1), jnp.int32)
        for bit in range(30, -1, -1):
            cand = thr | (1 << bit)
            cnt = jnp.sum(jnp.where(bits >= cand, 1.0, 0.0), axis=1, keepdims=True)
            thr = jnp.where(cnt >= cap, cand, thr)
        gt = jnp.where(bits > thr, 1.0, 0.0)
        eq = jnp.where(bits == thr, 1.0, 0.0)
        need = cap - jnp.sum(gt, axis=1, keepdims=True)
        sel = gt + eq * jnp.where(_excl_cumsum_lanes(eq) < need, 1.0, 0.0)
        pos = _excl_cumsum_lanes(sel)
        slot_ref[0] = jnp.where(sel > 0.0, pos, -1.0).astype(jnp.int32)
        gate_ref[0] = sel * aff


def _moe_route(h, mod, w_router_pad, cap):
    B, T, D = h.shape
    tm = _row_tile(T)
    shift, scale = mod
    E = N_EXPERTS
    return pl.pallas_call(
        functools.partial(_route_kernel, tm=tm, cap=cap, n_exp=E),
        grid=(B, T // tm),
        in_specs=[pl.BlockSpec((1, tm, D), lambda b, t: (b, t, 0)),
                  pl.BlockSpec((1, 1, D), lambda b, t: (b, 0, 0)),
                  pl.BlockSpec((1, 1, D), lambda b, t: (b, 0, 0)),
                  pl.BlockSpec((D, LANES), lambda b, t: (0, 0))],
        out_specs=[pl.BlockSpec((1, tm, D), lambda b, t: (b, t, 0)),
                   pl.BlockSpec((1, E, T), lambda b, t: (b, 0, 0)),
                   pl.BlockSpec((1, E, T), lambda b, t: (b, 0, 0))],
        out_shape=[jax.ShapeDtypeStruct((B, T, D), _bf16),
                   jax.ShapeDtypeStruct((B, E, T), jnp.int32),
                   jax.ShapeDtypeStruct((B, E, T), _f32)],
        scratch_shapes=[pltpu.VMEM((LANES, T), _f32)],
        compiler_params=_cparams(("arbitrary", "arbitrary")),
        name="moe_route",
    )(h, shift, scale, w_router_pad)


def _expert_kernel(xm_ref, slot_ref, gate_ref, wg_ref, wu_ref, wd_ref, ys_ref, *, bb, cap):
    T = xm_ref.shape[1]
    srow = lax.broadcasted_iota(jnp.int32, (cap, T), 0)
    xs, gates = [], []
    for i in range(bb):
        hit = slot_ref[i, 0] == srow
        xs.append(_dot(jnp.where(hit, 1.0, 0.0).astype(_bf16), xm_ref[i]).astype(_bf16))
        gates.append(jnp.sum(jnp.where(hit, gate_ref[i, 0], 0.0), axis=1, keepdims=True))
    xs = jnp.concatenate(xs, axis=0)
    hid = (_silu(_dot(xs, wg_ref[0])) * _dot(xs, wu_ref[0])).astype(_bf16)
    y = _dot(hid, wd_ref[0])
    for i in range(bb):
        ys_ref[i, 0] = (y[i * cap:(i + 1) * cap] * gates[i]).astype(_bf16)


def _moe_experts(xm, slot, gate, wg, wu, wd, cap, bb):
    B, T, D = xm.shape
    E, _, F = wg.shape
    slot4 = slot[:, :, None, :]
    gate4 = gate[:, :, None, :]
    once = pl.Buffered(1)
    return pl.pallas_call(
        functools.partial(_expert_kernel, bb=bb, cap=cap),
        grid=(E, B // bb),
        in_specs=[pl.BlockSpec((bb, T, D), lambda e, g: (g, 0, 0)),
                  pl.BlockSpec((bb, 1, 1, T), lambda e, g: (g, e, 0, 0)),
                  pl.BlockSpec((bb, 1, 1, T), lambda e, g: (g, e, 0, 0)),
                  pl.BlockSpec((1, D, F), lambda e, g: (e, 0, 0), pipeline_mode=once),
                  pl.BlockSpec((1, D, F), lambda e, g: (e, 0, 0), pipeline_mode=once),
                  pl.BlockSpec((1, F, D), lambda e, g: (e, 0, 0), pipeline_mode=once)],
        out_specs=pl.BlockSpec((bb, 1, cap, D), lambda e, g: (g, e, 0, 0)),
        out_shape=jax.ShapeDtypeStruct((B, E, cap, D), _bf16),
        compiler_params=_cparams(("arbitrary", "arbitrary")),
        name="moe_experts",
    )(xm, slot4, gate4, wg, wu, wd)


def _combine_kernel(slotc_ref, ys_ref, h_ref, ga_ref, g_ref, b_ref, o_ref, *, tm, cap, n_exp):
    col = lax.broadcasted_iota(jnp.int32, (tm, cap), 1)
    sc = slotc_ref[0]
    hot = [jnp.where(sc[:, e:e + 1] == col, 1.0, 0.0).astype(_bf16) for e in range(n_exp)]
    y = _dot(jnp.concatenate(hot, axis=1), ys_ref[0])
    o_ref[0] = _ln_rows(DN_ALPHA * h_ref[0] + ga_ref[0] * y, g_ref[...], b_ref[...], LN_EPS)


def _moe_combine(slot_col, ys, h, gate_vec, g, b, cap):
    B, T, D = h.shape
    E = N_EXPERTS
    tm = min(256, T)
    return pl.pallas_call(
        functools.partial(_combine_kernel, tm=tm, cap=cap, n_exp=E),
        grid=(B, T // tm),
        in_specs=[pl.BlockSpec((1, tm, E), lambda b_, t: (b_, t, 0)),
                  pl.BlockSpec((1, E * cap, D), lambda b_, t: (b_, 0, 0)),
                  pl.BlockSpec((1, tm, D), lambda b_, t: (b_, t, 0)),
                  pl.BlockSpec((1, 1, D), lambda b_, t: (b_, 0, 0)),
                  pl.BlockSpec((1, D), lambda b_, t: (0, 0)),
                  pl.BlockSpec((1, D), lambda b_, t: (0, 0))],
        out_specs=pl.BlockSpec((1, tm, D), lambda b_, t: (b_, t, 0)),
        out_shape=jax.ShapeDtypeStruct((B, T, D), _f32),
        compiler_params=_cparams(("arbitrary", "arbitrary")),
        name="moe_combine",
    )(slot_col, ys, h, gate_vec, g, b)


def _moe_block(h, mod2, gate_vec, ln_g, ln_b, w_router_pad, wg, wu, wd):
    B, T, D = h.shape
    cap = max(1, EC_CAPACITY * T // N_EXPERTS)
    xm, slot, gate = _moe_route(h, mod2, w_router_pad, cap)
    bb = max(1, min(B, 512 // cap))
    ys = _moe_experts(xm, slot, gate, wg, wu, wd, cap, bb)
    slot_col = jnp.swapaxes(slot, 1, 2)
    return _moe_combine(slot_col, ys.reshape(B, N_EXPERTS * cap, D), h, gate_vec, ln_g, ln_b, cap)


def _to_seq_major(t, dh):
    B, T, D = t.shape
    return jnp.transpose(t.reshape(B, T, D // dh, dh), (1, 3, 0, 2)).reshape(T, dh, B * (D // dh))


def _from_seq_major(t, B):
    T, dh, S = t.shape
    return jnp.transpose(t.reshape(T, dh, B, S // B), (2, 0, 3, 1)).reshape(B, T, (S // B) * dh)


def _head_param(p, B, dh):
    return jnp.tile(p.reshape(-1, dh).T, (1, B))


def _rwkv_stream(h, mod, p, s0, emit):
    B, T, D = h.shape
    dh = RWKV_HEAD_DIM
    outs = _rw_lora(h, mod, p["mu_prev3"], p["mu_next3"], p["w1c"], p["a1c"], p["g1"], p["w2p"], p["a2p"], p["g2"],
                    p["w0"], p["a0"], emit)
    lw, ia = outs[0], outs[1]
    k = _mm(h, p["w_k"], pro="mod_lerp", mod=mod, lerp=(p["mu_prev"][2:3], p["mu_next"][2:3]), name="rw_k")
    v = _mm(h, p["w_v"], pro="mod_lerp", mod=mod, lerp=(p["mu_prev"][3:4], p["mu_next"][3:4]), name="rw_v")
    rT = None
    if emit:
        r = _mm(h, p["w_r"], pro="mod_lerp", mod=mod, lerp=(p["mu_prev"][0:1], p["mu_next"][0:1]), name="rw_r")
        rT = _to_seq_major(r, dh)
    lwT = jnp.stack([_to_seq_major(lw[0], dh), _to_seq_major(lw[1], dh)])
    iaT = jnp.stack([_to_seq_major(ia[0], dh), _to_seq_major(ia[1], dh)])
    par = {n: _head_param(p[n], B, dh) for n in ("k_k", "k_a", "r_k", "gn_g", "gn_b")}
    yT, states = _rw_scan(lwT, iaT, _to_seq_major(k, dh), _to_seq_major(v, dh), rT, par, s0, emit)
    if not emit:
        return None, states
    return (_from_seq_major(yT[0], B), _from_seq_major(yT[1], B), outs[2]), states


def kernel(x, c, ctx, c_ctx, ada_w, ada_b, ln_g, ln_b, conv_w_in, conv_b_in, conv_w_dw, conv_b_dw, conv_ln_g, conv_ln_b, conv_w_out, conv_b_out, na_w_qkv, na_w_o, na_rpb, rw_mu_prev, rw_mu_next, rw_w_r, rw_w_k, rw_w_v, rw_w0, rw_w1, rw_w2, rw_a0, rw_a1, rw_a2, rw_k_k, rw_k_a, rw_r_k, rw_g1, rw_g2, rw_gn_g, rw_gn_b, rw_w_o, moe_router, moe_w_gate, moe_w_up, moe_w_down):
    B, T, D = x.shape
    L = ctx.shape[1]
    depth = ada_w.shape[0]
    E = moe_router.shape[-1]
    bf = lambda a: a.astype(_bf16)
    row = lambda a: a.reshape(1, -1)

    ctx_layers = [i for i in range(depth) if i % N_MIXERS != 0]
    last_ctx = max(ctx_layers) if ctx_layers else -1

    pad_rows = (-(B + 1)) % SUBLANES
    cond = jnp.concatenate([c, c_ctx[None, :], jnp.zeros((pad_rows, D), c.dtype)], axis=0)
    ada = _ada_all(cond, bf(ada_w), ada_b[:, None, :])

    def mods(i, is_ctx):
        m = ada[i, B:B + 1] if is_ctx else ada[i, :B]
        m = jnp.broadcast_to(m[:, None, :], (B, 1, 6 * D))
        return [m[:, :, k * D:(k + 1) * D] for k in range(6)]

    h, hc = x, ctx
    for i in range(depth):
        kind, j = i % N_MIXERS, i // N_MIXERS
        ctx_read = i <= last_ctx
        ctx_live = i < last_ctx
        sh1, sc1, ga1, sh2, sc2, ga2 = mods(i, False)
        if ctx_read:
            csh1, csc1, cga1, csh2, csc2, cga2 = mods(i, True)
        g1n, b1n = row(ln_g[i, 0]), row(ln_b[i, 0])
        g2n, b2n = row(ln_g[i, 1]), row(ln_b[i, 1])

        if kind == 0:
            w_in, w_out = bf(conv_w_in[j]), bf(conv_w_out[j])

            def conv_stream(hh, shift, scale, gate):
                u = _mm(hh, w_in, pro="mod", mod=(shift, scale), epi="glu", bias=row(conv_b_in[j]), name="conv_in")
                u = _dwconv(u, conv_w_dw[j], row(conv_b_dw[j]))
                return _mm(u, w_out, pro="ln_silu", ln=(row(conv_ln_g[j]), row(conv_ln_b[j])), bias=row(conv_b_out[j]),
                           epi="postnorm", post=(hh, gate, g1n, b1n), name="conv_out")

            h_new = conv_stream(h, sh1, sc1, ga1)
            hc_new = conv_stream(hc, csh1, csc1, cga1) if ctx_live else None
        elif kind == 1:
            w_qkv, w_o = bf(na_w_qkv[j]), bf(na_w_o[j])
            qkv = _mm(h, w_qkv, pro="mod", mod=(sh1, sc1), tn=D, name="na_qkv")
            qkv_c = _mm(hc, w_qkv, pro="mod", mod=(csh1, csc1), tn=D, name="na_qkv_ctx")
            tab = _na_bias_tables(na_rpb[j], T // GRID_W)
            per = LANES // (D // NA_HEADS)
            tab = tab.reshape(3, NA_HEADS // per, per, tab.shape[2], tab.shape[3])
            att = _na_attention(qkv, qkv_c, tab)
            h_new = _mm(att, w_o, epi="postnorm", post=(h, ga1, g1n, b1n), name="na_out")
            hc_new = None
            if ctx_live:
                att_c = _ctx_attention(qkv_c)
                hc_new = _mm(att_c, w_o, epi="postnorm", post=(hc, cga1, g1n, b1n), name="na_out_ctx")
        else:
            R = rw_w1.shape[-1]
            zw = jnp.zeros((R, D), _f32)
            p = {
                "mu_prev": rw_mu_prev[j], "mu_next": rw_mu_next[j],
                "mu_prev3": rw_mu_prev[j][jnp.array([1, 4, 5])][:, None, :],
                "mu_next3": rw_mu_next[j][jnp.array([1, 4, 5])][:, None, :],
                "w_r": bf(rw_w_r[j]), "w_k": bf(rw_w_k[j]), "w_v": bf(rw_w_v[j]),
                "w1c": bf(jnp.concatenate([rw_w1[j, 0], rw_w1[j, 1]], axis=1)),
                "a1c": bf(jnp.concatenate([rw_a1[j, 0], rw_a1[j, 1]], axis=1)),
                "g1": bf(rw_g1[j]), "g2": bf(rw_g2[j]),
                "w2p": bf(jnp.stack([jnp.concatenate([rw_w2[j, 0], zw]), jnp.concatenate([zw, rw_w2[j, 1]])])),
                "a2p": bf(jnp.stack([jnp.concatenate([rw_a2[j, 0], zw]), jnp.concatenate([zw, rw_a2[j, 1]])])),
                "w0": rw_w0[j][:, None, :], "a0": rw_a0[j][:, None, :],
                "k_k": rw_k_k[j], "k_a": rw_k_a[j], "r_k": rw_r_k[j].reshape(-1),
                "gn_g": rw_gn_g[j], "gn_b": rw_gn_b[j],
            }
            reads_c, states = _rwkv_stream(hc, (csh1, csc1), p, None, ctx_live)
            reads, _ = _rwkv_stream(h, (sh1, sc1), p, states, True)
            w_o = bf(rw_w_o[j])
            h_new = _mm(reads[0], w_o, pro="sum_gate", extra=(reads[1], reads[2]), epi="postnorm",
                        post=(h, ga1, g1n, b1n), name="rw_out")
            hc_new = None
            if ctx_live:
                hc_new = _mm(reads_c[0], w_o, pro="sum_gate", extra=(reads_c[1], reads_c[2]), epi="postnorm",
                             post=(hc, cga1, g1n, b1n), name="rw_out_ctx")

        wr = jnp.pad(moe_router[i], ((0, 0), (0, LANES - E)))
        wg, wu, wd = bf(moe_w_gate[i]), bf(moe_w_up[i]), bf(moe_w_down[i])
        h = _moe_block(h_new, (sh2, sc2), ga2, g2n, b2n, wr, wg, wu, wd)
        if ctx_live:
            hc = _moe_block(hc_new, (csh2, csc2), cga2, g2n, b2n, wr, wg, wu, wd)
    return h
```

```python
import functools

import numpy as np
import jax
import jax.numpy as jnp
from jax import lax
from jax.experimental import pallas as pl
from jax.experimental.pallas import tpu as pltpu

DEPTH = 4
N_MIXERS = 3
GRID_W = 64
CONV_WIDTH = 31
NA_HEADS = 16
NA_WIN_ROWS = 8
NA_WIN_COLS = 16
RWKV_HEAD_DIM = 64
N_EXPERTS = 16
EC_CAPACITY = 2
DN_ALPHA = (2 * DEPTH) ** 0.25
LN_EPS = 1e-5
GN_EPS = 64e-5

LANES = 128
SUBLANES = 8
VMEM_LIMIT = 56 * 1024 * 1024

NEG_BIG = -1e30

_f32 = jnp.float32
_bf16 = jnp.bfloat16


def _cparams(sem):
    return pltpu.CompilerParams(dimension_semantics=sem, vmem_limit_bytes=VMEM_LIMIT)


def _sigmoid(x):
    return 1.0 / (1.0 + jnp.exp(-x))


def _silu(x):
    return x * _sigmoid(x)


def _ln_rows(x, g, b, eps):
    mu = jnp.mean(x, axis=-1, keepdims=True)
    xc = x - mu
    var = jnp.mean(xc * xc, axis=-1, keepdims=True)
    return xc * lax.rsqrt(var + eps) * g + b


def _dot(a, b):
    return jnp.dot(a, b, preferred_element_type=_f32)


def _dot_nt(a, b):
    return lax.dot_general(a, b, (((1,), (1,)), ((), ())), preferred_element_type=_f32)


def _ada_kernel(c_ref, w_ref, b_ref, o_ref):
    s = _silu(c_ref[...]).astype(_bf16)
    o_ref[0] = _dot(s, w_ref[0]) + b_ref[0]


def _ada_all(cond, ada_w, ada_b):
    R, D = cond.shape
    L, _, N = ada_w.shape
    tn = D
    return pl.pallas_call(
        _ada_kernel,
        grid=(L, N // tn),
        in_specs=[pl.BlockSpec((R, D), lambda l, n: (0, 0)),
                  pl.BlockSpec((1, D, tn), lambda l, n: (l, 0, n)),
                  pl.BlockSpec((1, 1, tn), lambda l, n: (l, 0, n))],
        out_specs=pl.BlockSpec((1, R, tn), lambda l, n: (l, 0, n)),
        out_shape=jax.ShapeDtypeStruct((L, R, N), _f32),
        compiler_params=_cparams(("arbitrary", "arbitrary")),
        name="ada",
    )(cond, ada_w, ada_b)


def _mm_kernel(*refs, pro, epi, has_bias, tm):
    it = iter(refs)
    x_ref = next(it)
    x = x_ref[0]
    if pro in ("mod", "mod_lerp"):
        sh_ref, sc_ref = next(it), next(it)
        sh, sc1 = sh_ref[0], 1.0 + sc_ref[0]
        x = x * sc1 + sh
    if pro == "mod_lerp":
        hp_ref, hn_ref, mup_ref, mun_ref = next(it), next(it), next(it), next(it)
        t = pl.program_id(2)
        nt = pl.num_programs(2)
        prev_row = jnp.where(t == 0, 0.0, hp_ref[0, 0] * sc1 + sh)
        next_row = jnp.where(t == nt - 1, 0.0, hn_ref[0, 0] * sc1 + sh)
        rows = lax.broadcasted_iota(jnp.int32, x.shape, 0)
        x_prev = jnp.where(rows == 0, prev_row, pltpu.roll(x, 1, 0))
        x_next = jnp.where(rows == tm - 1, next_row, pltpu.roll(x, tm - 1, 0))
        x = x + (x_prev - x) * mup_ref[...] + (x_next - x) * mun_ref[...]
    if pro == "ln_silu":
        g_ref, b_ref = next(it), next(it)
        x = _silu(_ln_rows(x, g_ref[...], b_ref[...], LN_EPS))
    if pro == "sum_gate":
        x2_ref, xg_ref = next(it), next(it)
        x = (x + x2_ref[0]) * xg_ref[0]
    xb = x.astype(_bf16)
    w_ref = next(it)
    y = _dot(xb, w_ref[...])
    if epi == "glu":
        w2_ref = next(it)
        y2 = _dot(xb, w2_ref[...])
    if has_bias:
        y = y + next(it)[...]
        if epi == "glu":
            y2 = y2 + next(it)[...]
    if epi == "glu":
        y = y * _sigmoid(y2)
    if epi == "postnorm":
        h_ref, ga_ref, g_ref, b_ref = next(it), next(it), next(it), next(it)
        y = _ln_rows(DN_ALPHA * h_ref[0] + ga_ref[0] * y, g_ref[...], b_ref[...], LN_EPS)
    o_ref = next(it)
    o_ref[0] = y.astype(o_ref.dtype)


def _row_tile(T):
    return 512 if T % 512 == 0 else T


def _mm(x, w, *, pro=None, epi=None, bias=None, mod=None, lerp=None, ln=None, extra=None, post=None,
        tn=None, name="mm"):
    B, T, K = x.shape[-3:]
    N = w.shape[1]
    n_out = N // 2 if epi == "glu" else N
    tm = _row_tile(T)
    tn = n_out if tn is None else tn
    nt, nn = T // tm, n_out // tn
    grid = (nn, B, nt)
    xmap = lambda n, b, t: (b, t, 0)
    vecmap = lambda n, b, t: (b, 0, 0)
    if pro == "sum_gate":
        args = [x, x]
        specs = [pl.BlockSpec((None, 1, tm, K), lambda n, b, t: (0, b, t, 0)),
                 pl.BlockSpec((None, 1, tm, K), lambda n, b, t: (1, b, t, 0))]
    else:
        args, specs = [x], [pl.BlockSpec((1, tm, K), xmap)]
    if pro in ("mod", "mod_lerp"):
        shift, scale = mod
        args += [shift, scale]
        specs += [pl.BlockSpec((1, 1, K), vecmap)] * 2
    if pro == "mod_lerp":
        mu_prev, mu_next = lerp
        zero = jnp.zeros((B, 1, K), x.dtype)
        hp = jnp.concatenate([zero, x[:, tm - 1:T - 1:tm]], axis=1)[:, :, None]
        hn = jnp.concatenate([x[:, tm::tm], zero], axis=1)[:, :, None]
        args += [hp, hn, mu_prev, mu_next]
        specs += [pl.BlockSpec((1, 1, 1, K), lambda n, b, t: (b, t, 0, 0))] * 2
        specs += [pl.BlockSpec((1, K), lambda n, b, t: (0, 0))] * 2
    if pro == "ln_silu":
        args += list(ln)
        specs += [pl.BlockSpec((1, K), lambda n, b, t: (0, 0))] * 2
    if pro == "sum_gate":
        args.append(extra)
        specs.append(pl.BlockSpec((1, tm, K), xmap))
    args.append(w)
    specs.append(pl.BlockSpec((K, tn), lambda n, b, t: (0, n)))
    if epi == "glu":
        args.append(w)
        specs.append(pl.BlockSpec((K, tn), lambda n, b, t: (0, nn + n)))
    if bias is not None:
        args.append(bias)
        specs.append(pl.BlockSpec((1, tn), lambda n, b, t: (0, n)))
        if epi == "glu":
            args.append(bias)
            specs.append(pl.BlockSpec((1, tn), lambda n, b, t: (0, nn + n)))
    if epi == "postnorm":
        assert tn == n_out
        h, gate, g, b = post
        args += [h, gate, g, b]
        specs += [pl.BlockSpec((1, tm, n_out), xmap), pl.BlockSpec((1, 1, n_out), vecmap),
                  pl.BlockSpec((1, n_out), lambda n, b_, t: (0, 0)), pl.BlockSpec((1, n_out), lambda n, b_, t: (0, 0))]
    kern = functools.partial(_mm_kernel, pro=pro, epi=epi, has_bias=bias is not None, tm=tm)
    return pl.pallas_call(
        kern, grid=grid, in_specs=specs,
        out_specs=pl.BlockSpec((1, tm, tn), lambda n, b, t: (b, t, n)),
        out_shape=jax.ShapeDtypeStruct((B, T, n_out), _f32),
        compiler_params=_cparams(("arbitrary", "arbitrary", "arbitrary")),
        name=name,
    )(*args)


CONV_PAD = 16
CONV_ROWS = 64


def _dwconv_kernel(x_ref, w_ref, b_ref, o_ref, pad_ref, *, T):
    half = CONV_WIDTH // 2
    zeros = jnp.zeros((CONV_PAD, LANES), _f32)
    pad_ref[pl.ds(0, CONV_PAD), :] = zeros
    pad_ref[pl.ds(CONV_PAD + T, CONV_PAD), :] = zeros
    pad_ref[pl.ds(CONV_PAD, T), :] = x_ref[0]
    w = w_ref[...]
    bias = b_ref[...]

    def chunk(i, carry):
        r0 = pl.multiple_of(i * CONV_ROWS, CONV_ROWS)
        win = pad_ref.at[pl.ds(r0, CONV_ROWS + 2 * CONV_PAD), :]
        acc = jnp.zeros((CONV_ROWS, LANES), _f32) + bias
        for j in range(CONV_WIDTH):
            acc = acc + win[pl.ds(CONV_PAD - half + j, CONV_ROWS), :] * w[j:j + 1, :]
        o_ref[0, pl.ds(r0, CONV_ROWS), :] = acc
        return carry

    lax.fori_loop(0, T // CONV_ROWS, chunk, 0)


def _dwconv(u, w_dw, b_dw):
    B, T, C = u.shape
    return pl.pallas_call(
        functools.partial(_dwconv_kernel, T=T),
        grid=(B, C // LANES),
        in_specs=[pl.BlockSpec((1, T, LANES), lambda b, c: (b, 0, c)),
                  pl.BlockSpec((CONV_WIDTH, LANES), lambda b, c: (0, c)),
                  pl.BlockSpec((1, LANES), lambda b, c: (0, c))],
        out_specs=pl.BlockSpec((1, T, LANES), lambda b, c: (b, 0, c)),
        out_shape=jax.ShapeDtypeStruct((B, T, C), _f32),
        scratch_shapes=[pltpu.VMEM((T + 2 * CONV_PAD, LANES), _f32)],
        compiler_params=_cparams(("arbitrary", "arbitrary")),
        name="dwconv",
    )(u, w_dw, b_dw)


NA_Q_ROWS = 4
NA_BAND_ROWS = NA_Q_ROWS + NA_WIN_ROWS


def _na_bias_tables(rpb, rows):
    H = rpb.shape[0]
    W = GRID_W
    padded = jnp.pad(rpb, ((0, 0), (0, 0), (W, W)))
    toep = jnp.stack([padded[:, :, W + NA_WIN_COLS - 1 - qc: 2 * W + NA_WIN_COLS - 1 - qc] for qc in range(W)], axis=2)
    out = []
    for r_s in (0, NA_Q_ROWS, rows - NA_Q_ROWS):
        start = int(np.clip(r_s - NA_WIN_ROWS // 2, 0, rows - NA_BAND_ROWS))
        q_row = r_s + np.repeat(np.arange(NA_Q_ROWS), W)
        q_col = np.tile(np.arange(W), NA_Q_ROWS)
        k_row = start + np.repeat(np.arange(NA_BAND_ROWS), W)
        k_col = np.tile(np.arange(W), NA_BAND_ROWS)
        row0 = np.clip(q_row - NA_WIN_ROWS // 2, 0, rows - NA_WIN_ROWS)
        col0 = np.clip(q_col - NA_WIN_COLS // 2, 0, W - NA_WIN_COLS)
        ok = ((k_row[None, :] >= row0[:, None]) & (k_row[None, :] < row0[:, None] + NA_WIN_ROWS)
              & (k_col[None, :] >= col0[:, None]) & (k_col[None, :] < col0[:, None] + NA_WIN_COLS))
        dr = (start + np.arange(NA_BAND_ROWS))[None, :] - (r_s + np.arange(NA_Q_ROWS))[:, None]
        ridx = np.clip(dr + NA_WIN_ROWS - 1, 0, 2 * NA_WIN_ROWS - 2)
        slabs = jnp.stack([jnp.stack([toep[:, ridx[a, b]] for b in range(NA_BAND_ROWS)], axis=2)
                           for a in range(NA_Q_ROWS)], axis=1)
        bias = slabs.reshape(H, NA_Q_ROWS * W, NA_BAND_ROWS * W)
        out.append(jnp.where(jnp.asarray(ok)[None], bias, NEG_BIG))
    return jnp.stack(out)


def _na_kernel(q_ref, k_ref, v_ref, kc_ref, vc_ref, bias_ref, o_ref, *, rows, dh):
    rt = pl.program_id(2)
    n_rt = pl.num_programs(2)
    nq = NA_Q_ROWS * GRID_W
    nk = NA_BAND_ROWS * GRID_W
    start = jnp.clip(rt * NA_Q_ROWS - NA_WIN_ROWS // 2, 0, rows - NA_BAND_ROWS) * GRID_W
    start = pl.multiple_of(start, GRID_W)
    variant = jnp.where(rt == 0, 0, jnp.where(rt == n_rt - 1, 2, 1))
    q = q_ref[0] * (dh ** -0.5)
    k = k_ref[0, pl.ds(start, nk), :].astype(_bf16)
    v = v_ref[0, pl.ds(start, nk), :].astype(_bf16)
    kc = kc_ref[0].astype(_bf16)
    vc = vc_ref[0].astype(_bf16)
    lane = lax.broadcasted_iota(jnp.int32, (nq, LANES), 1)
    out = jnp.zeros((nq, LANES), _f32)
    for h in range(LANES // dh):
        in_head = (lane >= h * dh) & (lane < (h + 1) * dh)
        qh = jnp.where(in_head, q, 0.0).astype(_bf16)
        s_loc = _dot_nt(qh, k) + bias_ref[variant, 0, h]
        s_ctx = _dot_nt(qh, kc)
        m = jnp.maximum(jnp.max(s_loc, axis=-1, keepdims=True), jnp.max(s_ctx, axis=-1, keepdims=True))
        p_loc = jnp.exp(s_loc - m)
        p_ctx = jnp.exp(s_ctx - m)
        denom = jnp.sum(p_loc, axis=-1, keepdims=True) + jnp.sum(p_ctx, axis=-1, keepdims=True)
        o = (_dot(p_loc.astype(_bf16), v) + _dot(p_ctx.astype(_bf16), vc)) / denom
        out = jnp.where(in_head, o, out)
    o_ref[0] = out


def _na_attention(qkv, qkv_c, bias_tab):
    B, T, D3 = qkv.shape
    D = D3 // 3
    L = qkv_c.shape[1]
    rows = T // GRID_W
    dh = D // NA_HEADS
    ncb = D // LANES
    nq = NA_Q_ROWS * GRID_W
    nk = NA_BAND_ROWS * GRID_W
    per = LANES // dh
    return pl.pallas_call(
        functools.partial(_na_kernel, rows=rows, dh=dh),
        grid=(ncb, B, rows // NA_Q_ROWS),
        in_specs=[pl.BlockSpec((1, nq, LANES), lambda c, b, r: (b, r, c)),
                  pl.BlockSpec((1, T, LANES), lambda c, b, r: (b, 0, ncb + c)),
                  pl.BlockSpec((1, T, LANES), lambda c, b, r: (b, 0, 2 * ncb + c)),
                  pl.BlockSpec((1, L, LANES), lambda c, b, r: (b, 0, ncb + c)),
                  pl.BlockSpec((1, L, LANES), lambda c, b, r: (b, 0, 2 * ncb + c)),
                  pl.BlockSpec((3, 1, per, nq, nk), lambda c, b, r: (0, c, 0, 0, 0))],
        out_specs=pl.BlockSpec((1, nq, LANES), lambda c, b, r: (b, r, c)),
        out_shape=jax.ShapeDtypeStruct((B, T, D), _f32),
        compiler_params=_cparams(("arbitrary", "arbitrary", "arbitrary")),
        name="na_attn",
    )(qkv, qkv, qkv, qkv_c, qkv_c, bias_tab)


def _ctx_attn_kernel(q_ref, k_ref, v_ref, o_ref, *, dh):
    L = q_ref.shape[1]
    q = q_ref[0] * (dh ** -0.5)
    k = k_ref[0].astype(_bf16)
    v = v_ref[0].astype(_bf16)
    lane = lax.broadcasted_iota(jnp.int32, (L, LANES), 1)
    out = jnp.zeros((L, LANES), _f32)
    for h in range(LANES // dh):
        in_head = (lane >= h * dh) & (lane < (h + 1) * dh)
        qh = jnp.where(in_head, q, 0.0).astype(_bf16)
        s = _dot_nt(qh, k)
        p = jnp.exp(s - jnp.max(s, axis=-1, keepdims=True))
        o = _dot(p.astype(_bf16), v) / jnp.sum(p, axis=-1, keepdims=True)
        out = jnp.where(in_head, o, out)
    o_ref[0] = out


def _ctx_attention(qkv_c):
    B, L, D3 = qkv_c.shape
    D = D3 // 3
    ncb = D // LANES
    dh = D // NA_HEADS
    return pl.pallas_call(
        functools.partial(_ctx_attn_kernel, dh=dh),
        grid=(B, ncb),
        in_specs=[pl.BlockSpec((1, L, LANES), lambda b, c: (b, 0, c)),
                  pl.BlockSpec((1, L, LANES), lambda b, c: (b, 0, ncb + c)),
                  pl.BlockSpec((1, L, LANES), lambda b, c: (b, 0, 2 * ncb + c))],
        out_specs=pl.BlockSpec((1, L, LANES), lambda b, c: (b, 0, c)),
        out_shape=jax.ShapeDtypeStruct((B, L, D), _f32),
        compiler_params=_cparams(("arbitrary", "arbitrary")),
        name="ctx_attn",
    )(qkv_c, qkv_c, qkv_c)


def _rw_lora_kernel(x_ref, sh_ref, sc_ref, hp_ref, hn_ref, mup_ref, mun_ref, w1_ref, a1_ref, g1_ref,
                    w2_ref, a2_ref, g2_ref, w0_ref, a0_ref, lw_ref, ia_ref, *maybe_g, tm, emit):
    sh, sc1 = sh_ref[0], 1.0 + sc_ref[0]
    x = x_ref[0] * sc1 + sh
    t = pl.program_id(1)
    nt = pl.num_programs(1)
    prev_row = jnp.where(t == 0, 0.0, hp_ref[0, 0] * sc1 + sh)
    next_row = jnp.where(t == nt - 1, 0.0, hn_ref[0, 0] * sc1 + sh)
    rows = lax.broadcasted_iota(jnp.int32, x.shape, 0)
    d_prev = jnp.where(rows == 0, prev_row, pltpu.roll(x, 1, 0)) - x
    d_next = jnp.where(rows == tm - 1, next_row, pltpu.roll(x, tm - 1, 0)) - x

    def lerp(n):
        return (x + d_prev * mup_ref[n] + d_next * mun_ref[n]).astype(_bf16)

    tw = jnp.tanh(_dot(lerp(0), w1_ref[...])).astype(_bf16)
    ta = _dot(lerp(1), a1_ref[...]).astype(_bf16)
    for d in range(2):
        z = w0_ref[d] + _dot(tw, w2_ref[d])
        softplus_neg = jnp.maximum(-z, 0.0) + jnp.log(1.0 + jnp.exp(-jnp.abs(z)))
        lw_ref[d, 0] = -jnp.exp(-softplus_neg - 0.5)
        ia_ref[d, 0] = _sigmoid(a0_ref[d] + _dot(ta, a2_ref[d]))
    if emit:
        g_ref = maybe_g[0]
        tg = _sigmoid(_dot(lerp(2), g1_ref[...])).astype(_bf16)
        g_ref[0] = _dot(tg, g2_ref[...])


def _rw_lora(h, mod, mu_prev3, mu_next3, w1c, a1c, g1, w2p, a2p, g2, w0, a0, emit):
    B, T, D = h.shape
    tm = _row_tile(T)
    nt = T // tm
    shift, scale = mod
    zero = jnp.zeros((B, 1, D), h.dtype)
    hp = jnp.concatenate([zero, h[:, tm - 1:T - 1:tm]], axis=1)[:, :, None]
    hn = jnp.concatenate([h[:, tm::tm], zero], axis=1)[:, :, None]
    R2 = w1c.shape[1]
    G = g1.shape[1]
    c2 = lambda b, t: (0, 0)
    c3 = lambda b, t: (0, 0, 0)
    out_shape = [jax.ShapeDtypeStruct((2, B, T, D), _f32), jax.ShapeDtypeStruct((2, B, T, D), _f32)]
    out_specs = [pl.BlockSpec((2, 1, tm, D), lambda b, t: (0, b, t, 0))] * 2
    if emit:
        out_shape.append(jax.ShapeDtypeStruct((B, T, D), _f32))
        out_specs.append(pl.BlockSpec((1, tm, D), lambda b, t: (b, t, 0)))
    return pl.pallas_call(
        functools.partial(_rw_lora_kernel, tm=tm, emit=emit),
        grid=(B, nt),
        in_specs=[pl.BlockSpec((1, tm, D), lambda b, t: (b, t, 0)),
                  pl.BlockSpec((1, 1, D), lambda b, t: (b, 0, 0)),
                  pl.BlockSpec((1, 1, D), lambda b, t: (b, 0, 0)),
                  pl.BlockSpec((1, 1, 1, D), lambda b, t: (b, t, 0, 0)),
                  pl.BlockSpec((1, 1, 1, D), lambda b, t: (b, t, 0, 0)),
                  pl.BlockSpec((3, 1, D), c3), pl.BlockSpec((3, 1, D), c3),
                  pl.BlockSpec((D, R2), c2), pl.BlockSpec((D, R2), c2), pl.BlockSpec((D, G), c2),
                  pl.BlockSpec((2, R2, D), c3), pl.BlockSpec((2, R2, D), c3), pl.BlockSpec((G, D), c2),
                  pl.BlockSpec((2, 1, D), c3), pl.BlockSpec((2, 1, D), c3)],
        out_specs=out_specs, out_shape=out_shape,
        compiler_params=_cparams(("arbitrary", "arbitrary")),
        name="rw_lora",
    )(h, shift, scale, hp, hn, mu_prev3, mu_next3, w1c, a1c, g1, w2p, a2p, g2, w0, a0)


RW_TC = 32


def _rw_scan_kernel(*refs, emit, has_init, dh):
    it = iter(refs)
    lw_ref, ia_ref, k_ref, v_ref = next(it), next(it), next(it), next(it)
    r_ref = next(it) if emit else None
    kk_ref, ka_ref = next(it), next(it)
    if emit:
        rk_ref, gg_ref, gb_ref = next(it), next(it), next(it)
    s0_ref = next(it) if has_init else None
    y_ref = next(it) if emit else None
    sfin_ref = next(it)
    s_ref, w_s, a_s, b_s, kd_s = next(it), next(it), next(it), next(it), next(it)

    d = pl.program_id(0)
    c = pl.program_id(2)
    nc = pl.num_programs(2)

    @pl.when(c == 0)
    def _():
        if has_init:
            s_ref[...] = s0_ref[0, 0]
        else:
            s_ref[...] = jnp.zeros(s_ref.shape, _f32)

    def step(i, carry):
        t = jnp.where(d == 0, i, RW_TC - 1 - i)
        k_t = k_ref[t]
        ia_t = ia_ref[0, t]
        v_t = v_ref[t]
        kkr = k_t * kk_ref[...]
        kk = kkr * lax.rsqrt(jnp.maximum(jnp.sum(kkr * kkr, axis=0, keepdims=True), 1e-24))
        kd = k_t * (1.0 + (ia_t - 1.0) * ka_ref[...])
        w_s[...] = jnp.exp(lw_ref[0, t])
        a_s[...] = -kk
        b_s[...] = kk * ia_t
        kd_s[...] = kd
        part = [jnp.zeros((dh, LANES), _f32), jnp.zeros((dh, LANES), _f32)]
        for kj in range(dh):
            part[kj % 2] = part[kj % 2] + s_ref[kj] * a_s[kj:kj + 1, :]
        sa = part[0] + part[1]
        part = [jnp.zeros((dh, LANES), _f32), jnp.zeros((dh, LANES), _f32)]
        for kj in range(dh):
            sk = s_ref[kj] * w_s[kj:kj + 1, :] + sa * b_s[kj:kj + 1, :] + v_t * kd_s[kj:kj + 1, :]
            s_ref[kj] = sk
            if emit:
                part[kj % 2] = part[kj % 2] + sk * r_ref[t, pl.ds(kj, 1), :]
        if emit:
            r_t = r_ref[t]
            y = part[0] + part[1]
            mu = jnp.mean(y, axis=0, keepdims=True)
            yc = y - mu
            var = jnp.mean(yc * yc, axis=0, keepdims=True)
            bonus = jnp.sum(r_t * kd * rk_ref[...], axis=0, keepdims=True) * v_t
            y_ref[0, t] = yc * lax.rsqrt(var + GN_EPS) * gg_ref[...] + gb_ref[...] + bonus
        return carry

    lax.fori_loop(0, RW_TC, step, 0)

    @pl.when(c == nc - 1)
    def _():
        sfin_ref[0, 0] = s_ref[...]


def _rw_scan(lwT, iaT, kT, vT, rT, par, s0, emit):
    _, T, dh, S = lwT.shape
    nsb = S // LANES
    nc = T // RW_TC

    def tmap(d, s, c):
        return (jnp.where(d == 0, c, nc - 1 - c), 0, s)

    def dtmap(d, s, c):
        return (d, jnp.where(d == 0, c, nc - 1 - c), 0, s)

    tile = pl.BlockSpec((RW_TC, dh, LANES), tmap)
    dtile = pl.BlockSpec((1, RW_TC, dh, LANES), dtmap)
    ptile = pl.BlockSpec((dh, LANES), lambda d, s, c: (0, s))
    stile = pl.BlockSpec((1, 1, dh, dh, LANES), lambda d, s, c: (d, s, 0, 0, 0))
    args, specs = [lwT, iaT, kT, vT], [dtile, dtile, tile, tile]
    if emit:
        args.append(rT)
        specs.append(tile)
    args += [par["k_k"], par["k_a"]]
    specs += [ptile, ptile]
    if emit:
        args += [par["r_k"], par["gn_g"], par["gn_b"]]
        specs += [ptile] * 3
    if s0 is not None:
        args.append(s0)
        specs.append(stile)
    out_shape, out_specs = [], []
    if emit:
        out_shape.append(jax.ShapeDtypeStruct((2, T, dh, S), _f32))
        out_specs.append(dtile)
    out_shape.append(jax.ShapeDtypeStruct((2, nsb, dh, dh, LANES), _f32))
    out_specs.append(stile)
    res = pl.pallas_call(
        functools.partial(_rw_scan_kernel, emit=emit, has_init=s0 is not None, dh=dh),
        grid=(2, nsb, nc),
        in_specs=specs, out_specs=out_specs, out_shape=out_shape,
        scratch_shapes=[pltpu.VMEM((dh, dh, LANES), _f32)] + [pltpu.VMEM((dh, LANES), _f32)] * 4,
        compiler_params=_cparams(("arbitrary", "arbitrary", "arbitrary")),
        name="rw_scan_emit" if emit else "rw_scan_state",
    )(*args)
    return (res[0], res[1]) if emit else (None, res[0])


def _excl_cumsum_lanes(m):
    E, T = m.shape
    r = lax.broadcasted_iota(jnp.int32, (LANES, LANES), 0)
    c = lax.broadcasted_iota(jnp.int32, (LANES, LANES), 1)
    tri = jnp.where(r < c, 1.0, 0.0).astype(_bf16)
    out, offset = [], jnp.zeros((E, 1), _f32)
    for j in range(T // LANES):
        blk = m[:, j * LANES:(j + 1) * LANES]
        out.append(_dot(blk.astype(_bf16), tri) + offset)
        offset = offset + jnp.sum(blk, axis=1, keepdims=True)
    return jnp.concatenate(out, axis=1)


def _route_kernel(x_ref, sh_ref, sc_ref, wr_ref, xm_ref, slot_ref, gate_ref, lt_ref, *, tm, cap, n_exp):
    t = pl.program_id(1)
    nt = pl.num_programs(1)
    xm = x_ref[0] * (1.0 + sc_ref[0]) + sh_ref[0]
    xm_ref[0] = xm.astype(_bf16)
    logits = jnp.dot(xm, wr_ref[...], preferred_element_type=_f32, precision=lax.Precision.HIGHEST)
    lt_ref[:, pl.ds(pl.multiple_of(t * tm, tm), tm)] = logits.T

    @pl.when(t == nt - 1)
    def _():
        lg = lt_ref[0:n_exp, :]
        e = jnp.exp(lg - jnp.max(lg, axis=0, keepdims=True))
        aff = e / jnp.sum(e, axis=0, keepdims=True)
        bits = pltpu.bitcast(aff, jnp.int32)
        thr = jnp.zeros((n_exp, 1), jnp.int32)
        for bit in range(30, -1, -1):
            cand = thr | (1 << bit)
            cnt = jnp.sum(jnp.where(bits >= cand, 1.0, 0.0), axis=1, keepdims=True)
            thr = jnp.where(cnt >= cap, cand, thr)
        gt = jnp.where(bits > thr, 1.0, 0.0)
        eq = jnp.where(bits == thr, 1.0, 0.0)
        need = cap - jnp.sum(gt, axis=1, keepdims=True)
        sel = gt + eq * jnp.where(_excl_cumsum_lanes(eq) < need, 1.0, 0.0)
        pos = _excl_cumsum_lanes(sel)
        slot_ref[0] = jnp.where(sel > 0.0, pos, -1.0).astype(jnp.int32)
        gate_ref[0] = sel * aff


def _moe_route(h, mod, w_router_pad, cap):
    B, T, D = h.shape
    tm = _row_tile(T)
    shift, scale = mod
    E = N_EXPERTS
    return pl.pallas_call(
        functools.partial(_route_kernel, tm=tm, cap=cap, n_exp=E),
        grid=(B, T // tm),
        in_specs=[pl.BlockSpec((1, tm, D), lambda b, t: (b, t, 0)),
                  pl.BlockSpec((1, 1, D), lambda b, t: (b, 0, 0)),
                  pl.BlockSpec((1, 1, D), lambda b, t: (b, 0, 0)),
                  pl.BlockSpec((D, LANES), lambda b, t: (0, 0))],
        out_specs=[pl.BlockSpec((1, tm, D), lambda b, t: (b, t, 0)),
                   pl.BlockSpec((1, E, T), lambda b, t: (b, 0, 0)),
                   pl.BlockSpec((1, E, T), lambda b, t: (b, 0, 0))],
        out_shape=[jax.ShapeDtypeStruct((B, T, D), _bf16),
                   jax.ShapeDtypeStruct((B, E, T), jnp.int32),
                   jax.ShapeDtypeStruct((B, E, T), _f32)],
        scratch_shapes=[pltpu.VMEM((LANES, T), _f32)],
        compiler_params=_cparams(("arbitrary", "arbitrary")),
        name="moe_route",
    )(h, shift, scale, w_router_pad)


def _expert_kernel(xm_ref, slot_ref, gate_ref, wg_ref, wu_ref, wd_ref, ys_ref, *, bb, cap):
    T = xm_ref.shape[1]
    srow = lax.broadcasted_iota(jnp.int32, (cap, T), 0)
    xs, gates = [], []
    for i in range(bb):
        hit = slot_ref[i, 0] == srow
        xs.append(_dot(jnp.where(hit, 1.0, 0.0).astype(_bf16), xm_ref[i]).astype(_bf16))
        gates.append(jnp.sum(jnp.where(hit, gate_ref[i, 0], 0.0), axis=1, keepdims=True))
    xs = jnp.concatenate(xs, axis=0)
    hid = (_silu(_dot(xs, wg_ref[0])) * _dot(xs, wu_ref[0])).astype(_bf16)
    y = _dot(hid, wd_ref[0])
    for i in range(bb):
        ys_ref[i, 0] = (y[i * cap:(i + 1) * cap] * gates[i]).astype(_bf16)


def _moe_experts(xm, slot, gate, wg, wu, wd, cap, bb):
    B, T, D = xm.shape
    E, _, F = wg.shape
    slot4 = slot[:, :, None, :]
    gate4 = gate[:, :, None, :]
    once = pl.Buffered(1)
    return pl.pallas_call(
        functools.partial(_expert_kernel, bb=bb, cap=cap),
        grid=(E, B // bb),
        in_specs=[pl.BlockSpec((bb, T, D), lambda e, g: (g, 0, 0)),
                  pl.BlockSpec((bb, 1, 1, T), lambda e, g: (g, e, 0, 0)),
                  pl.BlockSpec((bb, 1, 1, T), lambda e, g: (g, e, 0, 0)),
                  pl.BlockSpec((1, D, F), lambda e, g: (e, 0, 0), pipeline_mode=once),
                  pl.BlockSpec((1, D, F), lambda e, g: (e, 0, 0), pipeline_mode=once),
                  pl.BlockSpec((1, F, D), lambda e, g: (e, 0, 0), pipeline_mode=once)],
        out_specs=pl.BlockSpec((bb, 1, cap, D), lambda e, g: (g, e, 0, 0)),
        out_shape=jax.ShapeDtypeStruct((B, E, cap, D), _bf16),
        compiler_params=_cparams(("arbitrary", "arbitrary")),
        name="moe_experts",
    )(xm, slot4, gate4, wg, wu, wd)


def _combine_kernel(slotc_ref, ys_ref, h_ref, ga_ref, g_ref, b_ref, o_ref, *, tm, cap, n_exp):
    col = lax.broadcasted_iota(jnp.int32, (tm, cap), 1)
    sc = slotc_ref[0]
    hot = [jnp.where(sc[:, e:e + 1] == col, 1.0, 0.0).astype(_bf16) for e in range(n_exp)]
    y = _dot(jnp.concatenate(hot, axis=1), ys_ref[0])
    o_ref[0] = _ln_rows(DN_ALPHA * h_ref[0] + ga_ref[0] * y, g_ref[...], b_ref[...], LN_EPS)


def _moe_combine(slot_col, ys, h, gate_vec, g, b, cap):
    B, T, D = h.shape
    E = N_EXPERTS
    tm = min(256, T)
    return pl.pallas_call(
        functools.partial(_combine_kernel, tm=tm, cap=cap, n_exp=E),
        grid=(B, T // tm),
        in_specs=[pl.BlockSpec((1, tm, E), lambda b_, t: (b_, t, 0)),
                  pl.BlockSpec((1, E * cap, D), lambda b_, t: (b_, 0, 0)),
                  pl.BlockSpec((1, tm, D), lambda b_, t: (b_, t, 0)),
                  pl.BlockSpec((1, 1, D), lambda b_, t: (b_, 0, 0)),
                  pl.BlockSpec((1, D), lambda b_, t: (0, 0)),
                  pl.BlockSpec((1, D), lambda b_, t: (0, 0))],
        out_specs=pl.BlockSpec((1, tm, D), lambda b_, t: (b_, t, 0)),
        out_shape=jax.ShapeDtypeStruct((B, T, D), _f32),
        compiler_params=_cparams(("arbitrary", "arbitrary")),
        name="moe_combine",
    )(slot_col, ys, h, gate_vec, g, b)


def _moe_block(h, mod2, gate_vec, ln_g, ln_b, w_router_pad, wg, wu, wd):
    B, T, D = h.shape
    cap = max(1, EC_CAPACITY * T // N_EXPERTS)
    xm, slot, gate = _moe_route(h, mod2, w_router_pad, cap)
    bb = max(1, min(B, 512 // cap))
    ys = _moe_experts(xm, slot, gate, wg, wu, wd, cap, bb)
    slot_col = jnp.swapaxes(slot, 1, 2)
    return _moe_combine(slot_col, ys.reshape(B, N_EXPERTS * cap, D), h, gate_vec, ln_g, ln_b, cap)


def _to_seq_major(t, dh):
    *lead, B, T, D = t.shape
    n = len(lead)
    perm = tuple(range(n)) + (n + 1, n + 3, n, n + 2)
    return jnp.transpose(t.reshape(*lead, B, T, D // dh, dh), perm).reshape(*lead, T, dh, B * (D // dh))


def _from_seq_major(t, B):
    *lead, T, dh, S = t.shape
    n = len(lead)
    perm = tuple(range(n)) + (n + 2, n, n + 3, n + 1)
    return jnp.transpose(t.reshape(*lead, T, dh, B, S // B), perm).reshape(*lead, B, T, (S // B) * dh)


def _head_param(p, B, dh):
    return jnp.tile(p.reshape(-1, dh).T, (1, B))


def _rwkv_stream(h, mod, p, s0, emit):
    B, T, D = h.shape
    dh = RWKV_HEAD_DIM
    outs = _rw_lora(h, mod, p["mu_prev3"], p["mu_next3"], p["w1c"], p["a1c"], p["g1"], p["w2p"], p["a2p"], p["g2"],
                    p["w0"], p["a0"], emit)
    lw, ia = outs[0], outs[1]
    k = _mm(h, p["w_k"], pro="mod_lerp", mod=mod, lerp=(p["mu_prev"][2:3], p["mu_next"][2:3]), name="rw_k")
    v = _mm(h, p["w_v"], pro="mod_lerp", mod=mod, lerp=(p["mu_prev"][3:4], p["mu_next"][3:4]), name="rw_v")
    rT = None
    if emit:
        r = _mm(h, p["w_r"], pro="mod_lerp", mod=mod, lerp=(p["mu_prev"][0:1], p["mu_next"][0:1]), name="rw_r")
        rT = _to_seq_major(r, dh)
    par = {n: _head_param(p[n], B, dh) for n in ("k_k", "k_a", "r_k", "gn_g", "gn_b")}
    yT, states = _rw_scan(_to_seq_major(lw, dh), _to_seq_major(ia, dh), _to_seq_major(k, dh), _to_seq_major(v, dh),
                          rT, par, s0, emit)
    if not emit:
        return None, states
    return (_from_seq_major(yT, B), outs[2]), states


def kernel(x, c, ctx, c_ctx, ada_w, ada_b, ln_g, ln_b, conv_w_in, conv_b_in, conv_w_dw, conv_b_dw, conv_ln_g, conv_ln_b, conv_w_out, conv_b_out, na_w_qkv, na_w_o, na_rpb, rw_mu_prev, rw_mu_next, rw_w_r, rw_w_k, rw_w_v, rw_w0, rw_w1, rw_w2, rw_a0, rw_a1, rw_a2, rw_k_k, rw_k_a, rw_r_k, rw_g1, rw_g2, rw_gn_g, rw_gn_b, rw_w_o, moe_router, moe_w_gate, moe_w_up, moe_w_down):
    B, T, D = x.shape
    L = ctx.shape[1]
    depth = ada_w.shape[0]
    E = moe_router.shape[-1]
    bf = lambda a: a.astype(_bf16)
    row = lambda a: a.reshape(1, -1)

    ctx_layers = [i for i in range(depth) if i % N_MIXERS != 0]
    last_ctx = max(ctx_layers) if ctx_layers else -1

    pad_rows = (-(B + 1)) % SUBLANES
    cond = jnp.concatenate([c, c_ctx[None, :], jnp.zeros((pad_rows, D), c.dtype)], axis=0)
    ada = _ada_all(cond, bf(ada_w), ada_b[:, None, :])

    def mods(i, is_ctx):
        m = ada[i, B:B + 1] if is_ctx else ada[i, :B]
        m = jnp.broadcast_to(m[:, None, :], (B, 1, 6 * D))
        return [m[:, :, k * D:(k + 1) * D] for k in range(6)]

    h, hc = x, ctx
    for i in range(depth):
        kind, j = i % N_MIXERS, i // N_MIXERS
        ctx_read = i <= last_ctx
        ctx_live = i < last_ctx
        sh1, sc1, ga1, sh2, sc2, ga2 = mods(i, False)
        if ctx_read:
            csh1, csc1, cga1, csh2, csc2, cga2 = mods(i, True)
        g1n, b1n = row(ln_g[i, 0]), row(ln_b[i, 0])
        g2n, b2n = row(ln_g[i, 1]), row(ln_b[i, 1])

        if kind == 0:
            w_in, w_out = bf(conv_w_in[j]), bf(conv_w_out[j])

            def conv_stream(hh, shift, scale, gate):
                u = _mm(hh, w_in, pro="mod", mod=(shift, scale), epi="glu", bias=row(conv_b_in[j]), name="conv_in")
                u = _dwconv(u, conv_w_dw[j], row(conv_b_dw[j]))
                return _mm(u, w_out, pro="ln_silu", ln=(row(conv_ln_g[j]), row(conv_ln_b[j])), bias=row(conv_b_out[j]),
                           epi="postnorm", post=(hh, gate, g1n, b1n), name="conv_out")

            h_new = conv_stream(h, sh1, sc1, ga1)
            hc_new = conv_stream(hc, csh1, csc1, cga1) if ctx_live else None
        elif kind == 1:
            w_qkv, w_o = bf(na_w_qkv[j]), bf(na_w_o[j])
            qkv = _mm(h, w_qkv, pro="mod", mod=(sh1, sc1), tn=D, name="na_qkv")
            qkv_c = _mm(hc, w_qkv, pro="mod", mod=(csh1, csc1), tn=D, name="na_qkv_ctx")
            tab = _na_bias_tables(na_rpb[j], T // GRID_W)
            per = LANES // (D // NA_HEADS)
            tab = tab.reshape(3, NA_HEADS // per, per, tab.shape[2], tab.shape[3])
            att = _na_attention(qkv, qkv_c, tab)
            h_new = _mm(att, w_o, epi="postnorm", post=(h, ga1, g1n, b1n), name="na_out")
            hc_new = None
            if ctx_live:
                att_c = _ctx_attention(qkv_c)
                hc_new = _mm(att_c, w_o, epi="postnorm", post=(hc, cga1, g1n, b1n), name="na_out_ctx")
        else:
            R = rw_w1.shape[-1]
            zw = jnp.zeros((R, D), _f32)
            p = {
                "mu_prev": rw_mu_prev[j], "mu_next": rw_mu_next[j],
                "mu_prev3": rw_mu_prev[j][jnp.array([1, 4, 5])][:, None, :],
                "mu_next3": rw_mu_next[j][jnp.array([1, 4, 5])][:, None, :],
                "w_r": bf(rw_w_r[j]), "w_k": bf(rw_w_k[j]), "w_v": bf(rw_w_v[j]),
                "w1c": bf(jnp.concatenate([rw_w1[j, 0], rw_w1[j, 1]], axis=1)),
                "a1c": bf(jnp.concatenate([rw_a1[j, 0], rw_a1[j, 1]], axis=1)),
                "g1": bf(rw_g1[j]), "g2": bf(rw_g2[j]),
                "w2p": bf(jnp.stack([jnp.concatenate([rw_w2[j, 0], zw]), jnp.concatenate([zw, rw_w2[j, 1]])])),
                "a2p": bf(jnp.stack([jnp.concatenate([rw_a2[j, 0], zw]), jnp.concatenate([zw, rw_a2[j, 1]])])),
                "w0": rw_w0[j][:, None, :], "a0": rw_a0[j][:, None, :],
                "k_k": rw_k_k[j], "k_a": rw_k_a[j], "r_k": rw_r_k[j].reshape(-1),
                "gn_g": rw_gn_g[j], "gn_b": rw_gn_b[j],
            }
            reads_c, states = _rwkv_stream(hc, (csh1, csc1), p, None, ctx_live)
            reads, _ = _rwkv_stream(h, (sh1, sc1), p, states, True)
            w_o = bf(rw_w_o[j])
            h_new = _mm(reads[0], w_o, pro="sum_gate", extra=reads[1], epi="postnorm",
                        post=(h, ga1, g1n, b1n), name="rw_out")
            hc_new = None
            if ctx_live:
                hc_new = _mm(reads_c[0], w_o, pro="sum_gate", extra=reads_c[1], epi="postnorm",
                             post=(hc, cga1, g1n, b1n), name="rw_out_ctx")

        wr = jnp.pad(moe_router[i], ((0, 0), (0, LANES - E)))
        wg, wu, wd = bf(moe_w_gate[i]), bf(moe_w_up[i]), bf(moe_w_down[i])
        h = _moe_block(h_new, (sh2, sc2), ga2, g2n, b2n, wr, wg, wu, wd)
        if ctx_live:
            hc = _moe_block(hc_new, (csh2, csc2), cga2, g2n, b2n, wr, wg, wu, wd)
    return h
```

```python
import functools

import numpy as np
import jax
import jax.numpy as jnp
from jax import lax
from jax.experimental import pallas as pl
from jax.experimental.pallas import tpu as pltpu

DEPTH = 4
N_MIXERS = 3
GRID_W = 64
CONV_WIDTH = 31
NA_HEADS = 16
NA_WIN_ROWS = 8
NA_WIN_COLS = 16
RWKV_HEAD_DIM = 64
N_EXPERTS = 16
EC_CAPACITY = 2
DN_ALPHA = (2 * DEPTH) ** 0.25
LN_EPS = 1e-5
GN_EPS = 64e-5

LANES = 128
SUBLANES = 8
VMEM_LIMIT = 56 * 1024 * 1024

NEG_BIG = -1e30

_f32 = jnp.float32
_bf16 = jnp.bfloat16


def _cparams(sem):
    return pltpu.CompilerParams(dimension_semantics=sem, vmem_limit_bytes=VMEM_LIMIT)


def _sigmoid(x):
    return 1.0 / (1.0 + jnp.exp(-x))


def _silu(x):
    return x * _sigmoid(x)


def _ln_rows(x, g, b, eps):
    mu = jnp.mean(x, axis=-1, keepdims=True)
    xc = x - mu
    var = jnp.mean(xc * xc, axis=-1, keepdims=True)
    return xc * lax.rsqrt(var + eps) * g + b


def _dot(a, b):
    return jnp.dot(a, b, preferred_element_type=_f32)


def _dot_nt(a, b):
    return lax.dot_general(a, b, (((1,), (1,)), ((), ())), preferred_element_type=_f32)


def _ada_kernel(c_ref, w_ref, b_ref, o_ref):
    s = _silu(c_ref[...]).astype(_bf16)
    o_ref[0] = _dot(s, w_ref[0]) + b_ref[0]


def _ada_all(cond, ada_w, ada_b):
    R, D = cond.shape
    L, _, N = ada_w.shape
    tn = D
    return pl.pallas_call(
        _ada_kernel,
        grid=(L, N // tn),
        in_specs=[pl.BlockSpec((R, D), lambda l, n: (0, 0)),
                  pl.BlockSpec((1, D, tn), lambda l, n: (l, 0, n)),
                  pl.BlockSpec((1, 1, tn), lambda l, n: (l, 0, n))],
        out_specs=pl.BlockSpec((1, R, tn), lambda l, n: (l, 0, n)),
        out_shape=jax.ShapeDtypeStruct((L, R, N), _f32),
        compiler_params=_cparams(("arbitrary", "arbitrary")),
        name="ada",
    )(cond, ada_w, ada_b)


def _mm_kernel(*refs, pro, epi, has_bias, tm):
    it = iter(refs)
    x_ref = next(it)
    x = x_ref[0]
    if pro in ("mod", "mod_lerp"):
        sh_ref, sc_ref = next(it), next(it)
        sh, sc1 = sh_ref[0], 1.0 + sc_ref[0]
        x = x * sc1 + sh
    if pro == "mod_lerp":
        hp_ref, hn_ref, mup_ref, mun_ref = next(it), next(it), next(it), next(it)
        t = pl.program_id(2)
        nt = pl.num_programs(2)
        prev_row = jnp.where(t == 0, 0.0, hp_ref[0, 0] * sc1 + sh)
        next_row = jnp.where(t == nt - 1, 0.0, hn_ref[0, 0] * sc1 + sh)
        rows = lax.broadcasted_iota(jnp.int32, x.shape, 0)
        x_prev = jnp.where(rows == 0, prev_row, pltpu.roll(x, 1, 0))
        x_next = jnp.where(rows == tm - 1, next_row, pltpu.roll(x, tm - 1, 0))
        x = x + (x_prev - x) * mup_ref[...] + (x_next - x) * mun_ref[...]
    if pro == "ln_silu":
        g_ref, b_ref = next(it), next(it)
        x = _silu(_ln_rows(x, g_ref[...], b_ref[...], LN_EPS))
    if pro == "sum_gate":
        x2_ref, xg_ref = next(it), next(it)
        x = (x + x2_ref[0]) * xg_ref[0]
    xb = x.astype(_bf16)
    w_ref = next(it)
    y = _dot(xb, w_ref[...])
    if epi == "glu":
        w2_ref = next(it)
        y2 = _dot(xb, w2_ref[...])
    if has_bias:
        y = y + next(it)[...]
        if epi == "glu":
            y2 = y2 + next(it)[...]
    if epi == "glu":
        y = y * _sigmoid(y2)
    if epi == "postnorm":
        h_ref, ga_ref, g_ref, b_ref = next(it), next(it), next(it), next(it)
        y = _ln_rows(DN_ALPHA * h_ref[0] + ga_ref[0] * y, g_ref[...], b_ref[...], LN_EPS)
    o_ref = next(it)
    o_ref[0] = y.astype(o_ref.dtype)


def _row_tile(T):
    return 512 if T % 512 == 0 else T


def _mm(x, w, *, pro=None, epi=None, bias=None, mod=None, lerp=None, ln=None, extra=None, post=None,
        tn=None, out_dtype=_f32, name="mm"):
    B, T, K = x.shape[-3:]
    stacked = w.ndim == 3
    N = w.shape[-1]
    n_out = N // 2 if epi == "glu" else N
    tm = _row_tile(T)
    tn = n_out if tn is None else tn
    nt, nn = T // tm, (w.shape[0] if stacked else n_out // tn)
    grid = (nn, B, nt)
    xmap = lambda n, b, t: (b, t, 0)
    vecmap = lambda n, b, t: (b, 0, 0)
    if pro == "sum_gate":
        args = [x, x]
        specs = [pl.BlockSpec((None, 1, tm, K), lambda n, b, t: (0, b, t, 0)),
                 pl.BlockSpec((None, 1, tm, K), lambda n, b, t: (1, b, t, 0))]
    else:
        args, specs = [x], [pl.BlockSpec((1, tm, K), xmap)]
    if pro in ("mod", "mod_lerp"):
        shift, scale = mod
        args += [shift, scale]
        specs += [pl.BlockSpec((1, 1, K), vecmap)] * 2
    if pro == "mod_lerp":
        mu_prev, mu_next = lerp
        zero = jnp.zeros((B, 1, K), x.dtype)
        hp = jnp.concatenate([zero, x[:, tm - 1:T - 1:tm]], axis=1)[:, :, None]
        hn = jnp.concatenate([x[:, tm::tm], zero], axis=1)[:, :, None]
        args += [hp, hn, mu_prev, mu_next]
        specs += [pl.BlockSpec((1, 1, 1, K), lambda n, b, t: (b, t, 0, 0))] * 2
        if stacked:
            specs += [pl.BlockSpec((None, 1, K), lambda n, b, t: (n, 0, 0))] * 2
        else:
            specs += [pl.BlockSpec((1, K), lambda n, b, t: (0, 0))] * 2
    if pro == "ln_silu":
        args += list(ln)
        specs += [pl.BlockSpec((1, K), lambda n, b, t: (0, 0))] * 2
    if pro == "sum_gate":
        args.append(extra)
        specs.append(pl.BlockSpec((1, tm, K), xmap))
    args.append(w)
    if stacked:
        specs.append(pl.BlockSpec((None, K, tn), lambda n, b, t: (n, 0, 0)))
    else:
        specs.append(pl.BlockSpec((K, tn), lambda n, b, t: (0, n)))
    if epi == "glu":
        args.append(w)
        specs.append(pl.BlockSpec((K, tn), lambda n, b, t: (0, nn + n)))
    if bias is not None:
        args.append(bias)
        specs.append(pl.BlockSpec((1, tn), lambda n, b, t: (0, n)))
        if epi == "glu":
            args.append(bias)
            specs.append(pl.BlockSpec((1, tn), lambda n, b, t: (0, nn + n)))
    if epi == "postnorm":
        assert tn == n_out
        h, gate, g, b = post
        args += [h, gate, g, b]
        specs += [pl.BlockSpec((1, tm, n_out), xmap), pl.BlockSpec((1, 1, n_out), vecmap),
                  pl.BlockSpec((1, n_out), lambda n, b_, t: (0, 0)), pl.BlockSpec((1, n_out), lambda n, b_, t: (0, 0))]
    kern = functools.partial(_mm_kernel, pro=pro, epi=epi, has_bias=bias is not None, tm=tm)
    if stacked:
        out_spec = pl.BlockSpec((None, 1, tm, tn), lambda n, b, t: (n, b, t, 0))
        out_shape = jax.ShapeDtypeStruct((nn, B, T, n_out), out_dtype)
    else:
        out_spec = pl.BlockSpec((1, tm, tn), lambda n, b, t: (b, t, n))
        out_shape = jax.ShapeDtypeStruct((B, T, n_out), out_dtype)
    return pl.pallas_call(
        kern, grid=grid, in_specs=specs,
        out_specs=out_spec, out_shape=out_shape,
        compiler_params=_cparams(("arbitrary", "arbitrary", "arbitrary")),
        name=name,
    )(*args)


CONV_PAD = 16
CONV_ROWS = 64


def _dwconv_kernel(x_ref, w_ref, b_ref, o_ref, pad_ref, *, T):
    half = CONV_WIDTH // 2
    zeros = jnp.zeros((CONV_PAD, LANES), _f32)
    pad_ref[pl.ds(0, CONV_PAD), :] = zeros
    pad_ref[pl.ds(CONV_PAD + T, CONV_PAD), :] = zeros
    pad_ref[pl.ds(CONV_PAD, T), :] = x_ref[0]
    w = w_ref[...]
    bias = b_ref[...]

    def chunk(i, carry):
        r0 = pl.multiple_of(i * CONV_ROWS, CONV_ROWS)
        win = pad_ref.at[pl.ds(r0, CONV_ROWS + 2 * CONV_PAD), :]
        acc = jnp.zeros((CONV_ROWS, LANES), _f32) + bias
        for j in range(CONV_WIDTH):
            acc = acc + win[pl.ds(CONV_PAD - half + j, CONV_ROWS), :] * w[j:j + 1, :]
        o_ref[0, pl.ds(r0, CONV_ROWS), :] = acc
        return carry

    lax.fori_loop(0, T // CONV_ROWS, chunk, 0)


def _dwconv(u, w_dw, b_dw):
    B, T, C = u.shape
    return pl.pallas_call(
        functools.partial(_dwconv_kernel, T=T),
        grid=(B, C // LANES),
        in_specs=[pl.BlockSpec((1, T, LANES), lambda b, c: (b, 0, c)),
                  pl.BlockSpec((CONV_WIDTH, LANES), lambda b, c: (0, c)),
                  pl.BlockSpec((1, LANES), lambda b, c: (0, c))],
        out_specs=pl.BlockSpec((1, T, LANES), lambda b, c: (b, 0, c)),
        out_shape=jax.ShapeDtypeStruct((B, T, C), _f32),
        scratch_shapes=[pltpu.VMEM((T + 2 * CONV_PAD, LANES), _f32)],
        compiler_params=_cparams(("arbitrary", "arbitrary")),
        name="dwconv",
    )(u, w_dw, b_dw)


NA_Q_ROWS = 4
NA_BAND_ROWS = NA_Q_ROWS + NA_WIN_ROWS


def _na_bias_tables(rpb, rows):
    H = rpb.shape[0]
    W = GRID_W
    padded = jnp.pad(rpb, ((0, 0), (0, 0), (W, W)))
    toep = jnp.stack([padded[:, :, W + NA_WIN_COLS - 1 - qc: 2 * W + NA_WIN_COLS - 1 - qc] for qc in range(W)], axis=2)
    out = []
    for r_s in (0, NA_Q_ROWS, rows - NA_Q_ROWS):
        start = int(np.clip(r_s - NA_WIN_ROWS // 2, 0, rows - NA_BAND_ROWS))
        q_row = r_s + np.repeat(np.arange(NA_Q_ROWS), W)
        q_col = np.tile(np.arange(W), NA_Q_ROWS)
        k_row = start + np.repeat(np.arange(NA_BAND_ROWS), W)
        k_col = np.tile(np.arange(W), NA_BAND_ROWS)
        row0 = np.clip(q_row - NA_WIN_ROWS // 2, 0, rows - NA_WIN_ROWS)
        col0 = np.clip(q_col - NA_WIN_COLS // 2, 0, W - NA_WIN_COLS)
        ok = ((k_row[None, :] >= row0[:, None]) & (k_row[None, :] < row0[:, None] + NA_WIN_ROWS)
              & (k_col[None, :] >= col0[:, None]) & (k_col[None, :] < col0[:, None] + NA_WIN_COLS))
        dr = (start + np.arange(NA_BAND_ROWS))[None, :] - (r_s + np.arange(NA_Q_ROWS))[:, None]
        ridx = np.clip(dr + NA_WIN_ROWS - 1, 0, 2 * NA_WIN_ROWS - 2)
        slabs = jnp.stack([jnp.stack([toep[:, ridx[a, b]] for b in range(NA_BAND_ROWS)], axis=2)
                           for a in range(NA_Q_ROWS)], axis=1)
        bias = slabs.reshape(H, NA_Q_ROWS * W, NA_BAND_ROWS * W)
        out.append(jnp.where(jnp.asarray(ok)[None], bias, NEG_BIG))
    return jnp.stack(out)


def _na_kernel(q_ref, k_ref, v_ref, kc_ref, vc_ref, bias_ref, o_ref, *, rows, dh):
    rt = pl.program_id(2)
    n_rt = pl.num_programs(2)
    nq = NA_Q_ROWS * GRID_W
    nk = NA_BAND_ROWS * GRID_W
    start = jnp.clip(rt * NA_Q_ROWS - NA_WIN_ROWS // 2, 0, rows - NA_BAND_ROWS) * GRID_W
    start = pl.multiple_of(start, GRID_W)
    variant = jnp.where(rt == 0, 0, jnp.where(rt == n_rt - 1, 2, 1))
    q = q_ref[0] * (dh ** -0.5)
    k = k_ref[0, pl.ds(start, nk), :].astype(_bf16)
    v = v_ref[0, pl.ds(start, nk), :].astype(_bf16)
    kc = kc_ref[0].astype(_bf16)
    vc = vc_ref[0].astype(_bf16)
    lane = lax.broadcasted_iota(jnp.int32, (nq, LANES), 1)
    out = jnp.zeros((nq, LANES), _f32)
    for h in range(LANES // dh):
        in_head = (lane >= h * dh) & (lane < (h + 1) * dh)
        qh = jnp.where(in_head, q, 0.0).astype(_bf16)
        s_loc = _dot_nt(qh, k) + bias_ref[variant, 0, h]
        s_ctx = _dot_nt(qh, kc)
        m = jnp.maximum(jnp.max(s_loc, axis=-1, keepdims=True), jnp.max(s_ctx, axis=-1, keepdims=True))
        p_loc = jnp.exp(s_loc - m)
        p_ctx = jnp.exp(s_ctx - m)
        denom = jnp.sum(p_loc, axis=-1, keepdims=True) + jnp.sum(p_ctx, axis=-1, keepdims=True)
        o = (_dot(p_loc.astype(_bf16), v) + _dot(p_ctx.astype(_bf16), vc)) / denom
        out = jnp.where(in_head, o, out)
    o_ref[0] = out.astype(o_ref.dtype)


def _na_attention(qkv, qkv_c, bias_tab):
    B, T, D3 = qkv.shape
    D = D3 // 3
    L = qkv_c.shape[1]
    rows = T // GRID_W
    dh = D // NA_HEADS
    ncb = D // LANES
    nq = NA_Q_ROWS * GRID_W
    nk = NA_BAND_ROWS * GRID_W
    per = LANES // dh
    return pl.pallas_call(
        functools.partial(_na_kernel, rows=rows, dh=dh),
        grid=(ncb, B, rows // NA_Q_ROWS),
        in_specs=[pl.BlockSpec((1, nq, LANES), lambda c, b, r: (b, r, c)),
                  pl.BlockSpec((1, T, LANES), lambda c, b, r: (b, 0, ncb + c)),
                  pl.BlockSpec((1, T, LANES), lambda c, b, r: (b, 0, 2 * ncb + c)),
                  pl.BlockSpec((1, L, LANES), lambda c, b, r: (b, 0, ncb + c)),
                  pl.BlockSpec((1, L, LANES), lambda c, b, r: (b, 0, 2 * ncb + c)),
                  pl.BlockSpec((3, 1, per, nq, nk), lambda c, b, r: (0, c, 0, 0, 0))],
        out_specs=pl.BlockSpec((1, nq, LANES), lambda c, b, r: (b, r, c)),
        out_shape=jax.ShapeDtypeStruct((B, T, D), _bf16),
        compiler_params=_cparams(("arbitrary", "arbitrary", "arbitrary")),
        name="na_attn",
    )(qkv, qkv, qkv, qkv_c, qkv_c, bias_tab)


def _ctx_attn_kernel(q_ref, k_ref, v_ref, o_ref, *, dh):
    L = q_ref.shape[1]
    q = q_ref[0] * (dh ** -0.5)
    k = k_ref[0].astype(_bf16)
    v = v_ref[0].astype(_bf16)
    lane = lax.broadcasted_iota(jnp.int32, (L, LANES), 1)
    out = jnp.zeros((L, LANES), _f32)
    for h in range(LANES // dh):
        in_head = (lane >= h * dh) & (lane < (h + 1) * dh)
        qh = jnp.where(in_head, q, 0.0).astype(_bf16)
        s = _dot_nt(qh, k)
        p = jnp.exp(s - jnp.max(s, axis=-1, keepdims=True))
        o = _dot(p.astype(_bf16), v) / jnp.sum(p, axis=-1, keepdims=True)
        out = jnp.where(in_head, o, out)
    o_ref[0] = out.astype(o_ref.dtype)


def _ctx_attention(qkv_c):
    B, L, D3 = qkv_c.shape
    D = D3 // 3
    ncb = D // LANES
    dh = D // NA_HEADS
    return pl.pallas_call(
        functools.partial(_ctx_attn_kernel, dh=dh),
        grid=(B, ncb),
        in_specs=[pl.BlockSpec((1, L, LANES), lambda b, c: (b, 0, c)),
                  pl.BlockSpec((1, L, LANES), lambda b, c: (b, 0, ncb + c)),
                  pl.BlockSpec((1, L, LANES), lambda b, c: (b, 0, 2 * ncb + c))],
        out_specs=pl.BlockSpec((1, L, LANES), lambda b, c: (b, 0, c)),
        out_shape=jax.ShapeDtypeStruct((B, L, D), _bf16),
        compiler_params=_cparams(("arbitrary", "arbitrary")),
        name="ctx_attn",
    )(qkv_c, qkv_c, qkv_c)


def _rw_lora_kernel(x_ref, sh_ref, sc_ref, hp_ref, hn_ref, mup_ref, mun_ref, w1_ref, a1_ref, g1_ref,
                    w2_ref, a2_ref, g2_ref, w0_ref, a0_ref, lw_ref, ia_ref, *maybe_g, tm, emit):
    sh, sc1 = sh_ref[0], 1.0 + sc_ref[0]
    x = x_ref[0] * sc1 + sh
    t = pl.program_id(1)
    nt = pl.num_programs(1)
    prev_row = jnp.where(t == 0, 0.0, hp_ref[0, 0] * sc1 + sh)
    next_row = jnp.where(t == nt - 1, 0.0, hn_ref[0, 0] * sc1 + sh)
    rows = lax.broadcasted_iota(jnp.int32, x.shape, 0)
    d_prev = jnp.where(rows == 0, prev_row, pltpu.roll(x, 1, 0)) - x
    d_next = jnp.where(rows == tm - 1, next_row, pltpu.roll(x, tm - 1, 0)) - x

    def lerp(n):
        return (x + d_prev * mup_ref[n] + d_next * mun_ref[n]).astype(_bf16)

    tw = jnp.tanh(_dot(lerp(0), w1_ref[...])).astype(_bf16)
    ta = _dot(lerp(1), a1_ref[...]).astype(_bf16)
    for d in range(2):
        z = w0_ref[d] + _dot(tw, w2_ref[d])
        softplus_neg = jnp.maximum(-z, 0.0) + jnp.log(1.0 + jnp.exp(-jnp.abs(z)))
        lw_ref[d, 0] = -jnp.exp(-softplus_neg - 0.5)
        ia_ref[d, 0] = _sigmoid(a0_ref[d] + _dot(ta, a2_ref[d]))
    if emit:
        g_ref = maybe_g[0]
        tg = _sigmoid(_dot(lerp(2), g1_ref[...])).astype(_bf16)
        g_ref[0] = _dot(tg, g2_ref[...])


def _rw_lora(h, mod, mu_prev3, mu_next3, w1c, a1c, g1, w2p, a2p, g2, w0, a0, emit):
    B, T, D = h.shape
    tm = _row_tile(T)
    nt = T // tm
    shift, scale = mod
    zero = jnp.zeros((B, 1, D), h.dtype)
    hp = jnp.concatenate([zero, h[:, tm - 1:T - 1:tm]], axis=1)[:, :, None]
    hn = jnp.concatenate([h[:, tm::tm], zero], axis=1)[:, :, None]
    R2 = w1c.shape[1]
    G = g1.shape[1]
    c2 = lambda b, t: (0, 0)
    c3 = lambda b, t: (0, 0, 0)
    out_shape = [jax.ShapeDtypeStruct((2, B, T, D), _f32), jax.ShapeDtypeStruct((2, B, T, D), _f32)]
    out_specs = [pl.BlockSpec((2, 1, tm, D), lambda b, t: (0, b, t, 0))] * 2
    if emit:
        out_shape.append(jax.ShapeDtypeStruct((B, T, D), _f32))
        out_specs.append(pl.BlockSpec((1, tm, D), lambda b, t: (b, t, 0)))
    return pl.pallas_call(
        functools.partial(_rw_lora_kernel, tm=tm, emit=emit),
        grid=(B, nt),
        in_specs=[pl.BlockSpec((1, tm, D), lambda b, t: (b, t, 0)),
                  pl.BlockSpec((1, 1, D), lambda b, t: (b, 0, 0)),
                  pl.BlockSpec((1, 1, D), lambda b, t: (b, 0, 0)),
                  pl.BlockSpec((1, 1, 1, D), lambda b, t: (b, t, 0, 0)),
                  pl.BlockSpec((1, 1, 1, D), lambda b, t: (b, t, 0, 0)),
                  pl.BlockSpec((3, 1, D), c3), pl.BlockSpec((3, 1, D), c3),
                  pl.BlockSpec((D, R2), c2), pl.BlockSpec((D, R2), c2), pl.BlockSpec((D, G), c2),
                  pl.BlockSpec((2, R2, D), c3), pl.BlockSpec((2, R2, D), c3), pl.BlockSpec((G, D), c2),
                  pl.BlockSpec((2, 1, D), c3), pl.BlockSpec((2, 1, D), c3)],
        out_specs=out_specs, out_shape=out_shape,
        compiler_params=_cparams(("arbitrary", "arbitrary")),
        name="rw_lora",
    )(h, shift, scale, hp, hn, mu_prev3, mu_next3, w1c, a1c, g1, w2p, a2p, g2, w0, a0)


RW_TC = 32


def _rw_scan_kernel(*refs, emit, has_init, dh):
    it = iter(refs)
    lw_ref, ia_ref, k_ref, v_ref = next(it), next(it), next(it), next(it)
    r_ref = next(it) if emit else None
    kk_ref, ka_ref = next(it), next(it)
    if emit:
        rk_ref, gg_ref, gb_ref = next(it), next(it), next(it)
    s0_ref = next(it) if has_init else None
    y_ref = next(it) if emit else None
    sfin_ref = next(it)
    s_ref, w_s, a_s, b_s, kd_s = next(it), next(it), next(it), next(it), next(it)

    d = pl.program_id(0)
    c = pl.program_id(2)
    nc = pl.num_programs(2)

    @pl.when(c == 0)
    def _():
        if has_init:
            s_ref[...] = s0_ref[0, 0]
        else:
            s_ref[...] = jnp.zeros(s_ref.shape, _f32)

    def step(i, carry):
        t = jnp.where(d == 0, i, RW_TC - 1 - i)
        k_t = k_ref[t]
        ia_t = ia_ref[0, t]
        v_t = v_ref[t]
        kkr = k_t * kk_ref[...]
        kk = kkr * lax.rsqrt(jnp.maximum(jnp.sum(kkr * kkr, axis=0, keepdims=True), 1e-24))
        kd = k_t * (1.0 + (ia_t - 1.0) * ka_ref[...])
        w_s[...] = jnp.exp(lw_ref[0, t])
        a_s[...] = -kk
        b_s[...] = kk * ia_t
        kd_s[...] = kd
        part = [jnp.zeros((dh, LANES), _f32), jnp.zeros((dh, LANES), _f32)]
        for kj in range(dh):
            part[kj % 2] = part[kj % 2] + s_ref[kj] * a_s[kj:kj + 1, :]
        sa = part[0] + part[1]
        part = [jnp.zeros((dh, LANES), _f32), jnp.zeros((dh, LANES), _f32)]
        for kj in range(dh):
            sk = s_ref[kj] * w_s[kj:kj + 1, :] + sa * b_s[kj:kj + 1, :] + v_t * kd_s[kj:kj + 1, :]
            s_ref[kj] = sk
            if emit:
                part[kj % 2] = part[kj % 2] + sk * r_ref[t, pl.ds(kj, 1), :]
        if emit:
            r_t = r_ref[t]
            y = part[0] + part[1]
            mu = jnp.mean(y, axis=0, keepdims=True)
            yc = y - mu
            var = jnp.mean(yc * yc, axis=0, keepdims=True)
            bonus = jnp.sum(r_t * kd * rk_ref[...], axis=0, keepdims=True) * v_t
            y_ref[0, t] = yc * lax.rsqrt(var + GN_EPS) * gg_ref[...] + gb_ref[...] + bonus
        return carry

    lax.fori_loop(0, RW_TC, step, 0)

    @pl.when(c == nc - 1)
    def _():
        sfin_ref[0, 0] = s_ref[...]


def _rw_scan(lwT, iaT, kvrT, par, s0, emit):
    _, T, dh, S = lwT.shape
    nsb = S // LANES
    nc = T // RW_TC

    def tmap(d, s, c):
        return (jnp.where(d == 0, c, nc - 1 - c), 0, s)

    def dtmap(d, s, c):
        return (d, jnp.where(d == 0, c, nc - 1 - c), 0, s)

    def tile(g):
        return pl.BlockSpec((None, RW_TC, dh, LANES), lambda d, s, c: (g,) + tmap(d, s, c))

    dtile = pl.BlockSpec((1, RW_TC, dh, LANES), dtmap)
    ptile = pl.BlockSpec((dh, LANES), lambda d, s, c: (0, s))
    stile = pl.BlockSpec((1, 1, dh, dh, LANES), lambda d, s, c: (d, s, 0, 0, 0))
    args, specs = [lwT, iaT, kvrT, kvrT], [dtile, dtile, tile(0), tile(1)]
    if emit:
        args.append(kvrT)
        specs.append(tile(2))
    args += [par["k_k"], par["k_a"]]
    specs += [ptile, ptile]
    if emit:
        args += [par["r_k"], par["gn_g"], par["gn_b"]]
        specs += [ptile] * 3
    if s0 is not None:
        args.append(s0)
        specs.append(stile)
    out_shape, out_specs = [], []
    if emit:
        out_shape.append(jax.ShapeDtypeStruct((2, T, dh, S), _f32))
        out_specs.append(dtile)
    out_shape.append(jax.ShapeDtypeStruct((2, nsb, dh, dh, LANES), _f32))
    out_specs.append(stile)
    res = pl.pallas_call(
        functools.partial(_rw_scan_kernel, emit=emit, has_init=s0 is not None, dh=dh),
        grid=(2, nsb, nc),
        in_specs=specs, out_specs=out_specs, out_shape=out_shape,
        scratch_shapes=[pltpu.VMEM((dh, dh, LANES), _f32)] + [pltpu.VMEM((dh, LANES), _f32)] * 4,
        compiler_params=_cparams(("arbitrary", "arbitrary", "arbitrary")),
        name="rw_scan_emit" if emit else "rw_scan_state",
    )(*args)
    return (res[0], res[1]) if emit else (None, res[0])


def _excl_cumsum_lanes(m):
    E, T = m.shape
    r = lax.broadcasted_iota(jnp.int32, (LANES, LANES), 0)
    c = lax.broadcasted_iota(jnp.int32, (LANES, LANES), 1)
    tri = jnp.where(r < c, 1.0, 0.0).astype(_bf16)
    out, offset = [], jnp.zeros((E, 1), _f32)
    for j in range(T // LANES):
        blk = m[:, j * LANES:(j + 1) * LANES]
        out.append(_dot(blk.astype(_bf16), tri) + offset)
        offset = offset + jnp.sum(blk, axis=1, keepdims=True)
    return jnp.concatenate(out, axis=1)


def _route_kernel(x_ref, sh_ref, sc_ref, wr_ref, xm_ref, slot_ref, gate_ref, lt_ref, *, tm, cap, n_exp):
    t = pl.program_id(1)
    nt = pl.num_programs(1)
    xm = x_ref[0] * (1.0 + sc_ref[0]) + sh_ref[0]
    xm_ref[0] = xm.astype(_bf16)
    logits = jnp.dot(xm, wr_ref[...], preferred_element_type=_f32, precision=lax.Precision.HIGHEST)
    lt_ref[:, pl.ds(pl.multiple_of(t * tm, tm), tm)] = logits.T

    @pl.when(t == nt - 1)
    def _():
        lg = lt_ref[0:n_exp, :]
        e = jnp.exp(lg - jnp.max(lg, axis=0, keepdims=True))
        aff = e / jnp.sum(e, axis=0, keepdims=True)
        bits = pltpu.bitcast(aff, jnp.int32)
        thr = jnp.zeros((n_exp, 1), jnp.int32)
        for bit in range(30, -1, -1):
            cand = thr | (1 << bit)
            cnt = jnp.sum(jnp.where(bits >= cand, 1.0, 0.0), axis=1, keepdims=True)
            thr = jnp.where(cnt >= cap, cand, thr)
        gt = jnp.where(bits > thr, 1.0, 0.0)
        eq = jnp.where(bits == thr, 1.0, 0.0)
        need = cap - jnp.sum(gt, axis=1, keepdims=True)
        sel = gt + eq * jnp.where(_excl_cumsum_lanes(eq) < need, 1.0, 0.0)
        pos = _excl_cumsum_lanes(sel)
        slot_ref[0] = jnp.where(sel > 0.0, pos, -1.0).astype(jnp.int32)
        gate_ref[0] = sel * aff


def _moe_route(h, mod, w_router_pad, cap):
    B, T, D = h.shape
    tm = _row_tile(T)
    shift, scale = mod
    E = N_EXPERTS
    return pl.pallas_call(
        functools.partial(_route_kernel, tm=tm, cap=cap, n_exp=E),
        grid=(B, T // tm),
        in_specs=[pl.BlockSpec((1, tm, D), lambda b, t: (b, t, 0)),
                  pl.BlockSpec((1, 1, D), lambda b, t: (b, 0, 0)),
                  pl.BlockSpec((1, 1, D), lambda b, t: (b, 0, 0)),
                  pl.BlockSpec((D, LANES), lambda b, t: (0, 0))],
        out_specs=[pl.BlockSpec((1, tm, D), lambda b, t: (b, t, 0)),
                   pl.BlockSpec((1, E, T), lambda b, t: (b, 0, 0)),
                   pl.BlockSpec((1, E, T), lambda b, t: (b, 0, 0))],
        out_shape=[jax.ShapeDtypeStruct((B, T, D), _bf16),
                   jax.ShapeDtypeStruct((B, E, T), jnp.int32),
                   jax.ShapeDtypeStruct((B, E, T), _f32)],
        scratch_shapes=[pltpu.VMEM((LANES, T), _f32)],
        compiler_params=_cparams(("arbitrary", "arbitrary")),
        name="moe_route",
    )(h, shift, scale, w_router_pad)


def _expert_kernel(xm_ref, slot_ref, gate_ref, wg_ref, wu_ref, wd_ref, ys_ref, *, bb, cap):
    T = xm_ref.shape[1]
    srow = lax.broadcasted_iota(jnp.int32, (cap, T), 0)
    xs, gates = [], []
    for i in range(bb):
        hit = slot_ref[i, 0] == srow
        xs.append(_dot(jnp.where(hit, 1.0, 0.0).astype(_bf16), xm_ref[i]).astype(_bf16))
        gates.append(jnp.sum(jnp.where(hit, gate_ref[i, 0], 0.0), axis=1, keepdims=True))
    xs = jnp.concatenate(xs, axis=0)
    hid = (_silu(_dot(xs, wg_ref[0])) * _dot(xs, wu_ref[0])).astype(_bf16)
    y = _dot(hid, wd_ref[0])
    for i in range(bb):
        ys_ref[i, 0] = (y[i * cap:(i + 1) * cap] * gates[i]).astype(_bf16)


def _moe_experts(xm, slot, gate, wg, wu, wd, layer, cap, bb):
    B, T, D = xm.shape
    _, E, _, F = wg.shape
    slot4 = slot[:, :, None, :]
    gate4 = gate[:, :, None, :]
    once = pl.Buffered(1)
    return pl.pallas_call(
        functools.partial(_expert_kernel, bb=bb, cap=cap),
        grid=(E, B // bb),
        in_specs=[pl.BlockSpec((bb, T, D), lambda e, g: (g, 0, 0)),
                  pl.BlockSpec((bb, 1, 1, T), lambda e, g: (g, e, 0, 0)),
                  pl.BlockSpec((bb, 1, 1, T), lambda e, g: (g, e, 0, 0)),
                  pl.BlockSpec((None, 1, D, F), lambda e, g: (layer, e, 0, 0), pipeline_mode=once),
                  pl.BlockSpec((None, 1, D, F), lambda e, g: (layer, e, 0, 0), pipeline_mode=once),
                  pl.BlockSpec((None, 1, F, D), lambda e, g: (layer, e, 0, 0), pipeline_mode=once)],
        out_specs=pl.BlockSpec((bb, 1, cap, D), lambda e, g: (g, e, 0, 0)),
        out_shape=jax.ShapeDtypeStruct((B, E, cap, D), _bf16),
        compiler_params=_cparams(("arbitrary", "arbitrary")),
        name="moe_experts",
    )(xm, slot4, gate4, wg, wu, wd)


def _combine_kernel(slotc_ref, ys_ref, h_ref, ga_ref, g_ref, b_ref, o_ref, *, tm, cap, n_exp):
    col = lax.broadcasted_iota(jnp.int32, (tm, cap), 1)
    sc = slotc_ref[0]
    hot = [jnp.where(sc[:, e:e + 1] == col, 1.0, 0.0).astype(_bf16) for e in range(n_exp)]
    y = _dot(jnp.concatenate(hot, axis=1), ys_ref[0])
    o_ref[0] = _ln_rows(DN_ALPHA * h_ref[0] + ga_ref[0] * y, g_ref[...], b_ref[...], LN_EPS)


def _moe_combine(slot_col, ys, h, gate_vec, g, b, cap):
    B, T, D = h.shape
    E = N_EXPERTS
    tm = _row_tile(T)
    return pl.pallas_call(
        functools.partial(_combine_kernel, tm=tm, cap=cap, n_exp=E),
        grid=(B, T // tm),
        in_specs=[pl.BlockSpec((1, tm, E), lambda b_, t: (b_, t, 0)),
                  pl.BlockSpec((1, E * cap, D), lambda b_, t: (b_, 0, 0)),
                  pl.BlockSpec((1, tm, D), lambda b_, t: (b_, t, 0)),
                  pl.BlockSpec((1, 1, D), lambda b_, t: (b_, 0, 0)),
                  pl.BlockSpec((1, D), lambda b_, t: (0, 0)),
                  pl.BlockSpec((1, D), lambda b_, t: (0, 0))],
        out_specs=pl.BlockSpec((1, tm, D), lambda b_, t: (b_, t, 0)),
        out_shape=jax.ShapeDtypeStruct((B, T, D), _f32),
        compiler_params=_cparams(("arbitrary", "arbitrary")),
        name="moe_combine",
    )(slot_col, ys, h, gate_vec, g, b)


def _moe_block(h, mod2, gate_vec, ln_g, ln_b, w_router_pad, wg, wu, wd, layer):
    B, T, D = h.shape
    cap = max(1, EC_CAPACITY * T // N_EXPERTS)
    xm, slot, gate = _moe_route(h, mod2, w_router_pad, cap)
    bb = max(1, min(B, 512 // cap))
    ys = _moe_experts(xm, slot, gate, wg, wu, wd, layer, cap, bb)
    slot_col = jnp.swapaxes(slot, 1, 2)
    return _moe_combine(slot_col, ys.reshape(B, N_EXPERTS * cap, D), h, gate_vec, ln_g, ln_b, cap)


def _to_seq_major(t, dh):
    *lead, B, T, D = t.shape
    n = len(lead)
    perm = tuple(range(n)) + (n + 1, n + 3, n, n + 2)
    return jnp.transpose(t.reshape(*lead, B, T, D // dh, dh), perm).reshape(*lead, T, dh, B * (D // dh))


def _from_seq_major(t, B):
    *lead, T, dh, S = t.shape
    n = len(lead)
    perm = tuple(range(n)) + (n + 2, n, n + 3, n + 1)
    return jnp.transpose(t.reshape(*lead, T, dh, B, S // B), perm).reshape(*lead, B, T, (S // B) * dh)


def _head_param(p, B, dh):
    return jnp.tile(p.reshape(-1, dh).T, (1, B))


def _rwkv_stream(h, mod, p, s0, emit):
    B, T, D = h.shape
    dh = RWKV_HEAD_DIM
    outs = _rw_lora(h, mod, p["mu_prev3"], p["mu_next3"], p["w1c"], p["a1c"], p["g1"], p["w2p"], p["a2p"], p["g2"],
                    p["w0"], p["a0"], emit)
    lw, ia = outs[0], outs[1]
    n_proj = 3 if emit else 2
    kvr = _mm(h, p["w_kvr"][:n_proj], pro="mod_lerp", mod=mod,
              lerp=(p["mu_prev_kvr"][:n_proj], p["mu_next_kvr"][:n_proj]), name="rw_kvr")
    par = {n: _head_param(p[n], B, dh) for n in ("k_k", "k_a", "r_k", "gn_g", "gn_b")}
    yT, states = _rw_scan(_to_seq_major(lw, dh), _to_seq_major(ia, dh), _to_seq_major(kvr, dh), par, s0, emit)
    if not emit:
        return None, states
    return (_from_seq_major(yT, B), outs[2]), states


def kernel(x, c, ctx, c_ctx, ada_w, ada_b, ln_g, ln_b, conv_w_in, conv_b_in, conv_w_dw, conv_b_dw, conv_ln_g, conv_ln_b, conv_w_out, conv_b_out, na_w_qkv, na_w_o, na_rpb, rw_mu_prev, rw_mu_next, rw_w_r, rw_w_k, rw_w_v, rw_w0, rw_w1, rw_w2, rw_a0, rw_a1, rw_a2, rw_k_k, rw_k_a, rw_r_k, rw_g1, rw_g2, rw_gn_g, rw_gn_b, rw_w_o, moe_router, moe_w_gate, moe_w_up, moe_w_down):
    B, T, D = x.shape
    L = ctx.shape[1]
    depth = ada_w.shape[0]
    E = moe_router.shape[-1]
    bf = lambda a: a.astype(_bf16)
    row = lambda a: a.reshape(1, -1)

    ctx_layers = [i for i in range(depth) if i % N_MIXERS != 0]
    last_ctx = max(ctx_layers) if ctx_layers else -1

    pad_rows = (-(B + 1)) % SUBLANES
    cond = jnp.concatenate([c, c_ctx[None, :], jnp.zeros((pad_rows, D), c.dtype)], axis=0)
    ada = _ada_all(cond, bf(ada_w), ada_b[:, None, :])

    def mods(i, is_ctx):
        m = ada[i, B:B + 1] if is_ctx else ada[i, :B]
        m = jnp.broadcast_to(m[:, None, :], (B, 1, 6 * D))
        return [m[:, :, k * D:(k + 1) * D] for k in range(6)]

    wg, wu, wd = bf(moe_w_gate), bf(moe_w_up), bf(moe_w_down)
    h, hc = x, ctx
    for i in range(depth):
        kind, j = i % N_MIXERS, i // N_MIXERS
        ctx_read = i <= last_ctx
        ctx_live = i < last_ctx
        sh1, sc1, ga1, sh2, sc2, ga2 = mods(i, False)
        if ctx_read:
            csh1, csc1, cga1, csh2, csc2, cga2 = mods(i, True)
        g1n, b1n = row(ln_g[i, 0]), row(ln_b[i, 0])
        g2n, b2n = row(ln_g[i, 1]), row(ln_b[i, 1])

        if kind == 0:
            w_in, w_out = bf(conv_w_in[j]), bf(conv_w_out[j])

            def conv_stream(hh, shift, scale, gate):
                u = _mm(hh, w_in, pro="mod", mod=(shift, scale), epi="glu", bias=row(conv_b_in[j]), name="conv_in")
                u = _dwconv(u, conv_w_dw[j], row(conv_b_dw[j]))
                return _mm(u, w_out, pro="ln_silu", ln=(row(conv_ln_g[j]), row(conv_ln_b[j])), bias=row(conv_b_out[j]),
                           epi="postnorm", post=(hh, gate, g1n, b1n), name="conv_out")

            h_new = conv_stream(h, sh1, sc1, ga1)
            hc_new = conv_stream(hc, csh1, csc1, cga1) if ctx_live else None
        elif kind == 1:
            w_qkv, w_o = bf(na_w_qkv[j]), bf(na_w_o[j])
            qkv = _mm(h, w_qkv, pro="mod", mod=(sh1, sc1), tn=D, out_dtype=_bf16, name="na_qkv")
            qkv_c = _mm(hc, w_qkv, pro="mod", mod=(csh1, csc1), tn=D, out_dtype=_bf16, name="na_qkv_ctx")
            tab = _na_bias_tables(na_rpb[j], T // GRID_W)
            per = LANES // (D // NA_HEADS)
            tab = tab.reshape(3, NA_HEADS // per, per, tab.shape[2], tab.shape[3])
            att = _na_attention(qkv, qkv_c, tab)
            h_new = _mm(att, w_o, epi="postnorm", post=(h, ga1, g1n, b1n), name="na_out")
            hc_new = None
            if ctx_live:
                att_c = _ctx_attention(qkv_c)
                hc_new = _mm(att_c, w_o, epi="postnorm", post=(hc, cga1, g1n, b1n), name="na_out_ctx")
        else:
            R = rw_w1.shape[-1]
            zw = jnp.zeros((R, D), _f32)
            p = {
                "mu_prev3": rw_mu_prev[j][jnp.array([1, 4, 5])][:, None, :],
                "mu_next3": rw_mu_next[j][jnp.array([1, 4, 5])][:, None, :],
                "mu_prev_kvr": rw_mu_prev[j][jnp.array([2, 3, 0])][:, None, :],
                "mu_next_kvr": rw_mu_next[j][jnp.array([2, 3, 0])][:, None, :],
                "w_kvr": bf(jnp.stack([rw_w_k[j], rw_w_v[j], rw_w_r[j]])),
                "w1c": bf(jnp.concatenate([rw_w1[j, 0], rw_w1[j, 1]], axis=1)),
                "a1c": bf(jnp.concatenate([rw_a1[j, 0], rw_a1[j, 1]], axis=1)),
                "g1": bf(rw_g1[j]), "g2": bf(rw_g2[j]),
                "w2p": bf(jnp.stack([jnp.concatenate([rw_w2[j, 0], zw]), jnp.concatenate([zw, rw_w2[j, 1]])])),
                "a2p": bf(jnp.stack([jnp.concatenate([rw_a2[j, 0], zw]), jnp.concatenate([zw, rw_a2[j, 1]])])),
                "w0": rw_w0[j][:, None, :], "a0": rw_a0[j][:, None, :],
                "k_k": rw_k_k[j], "k_a": rw_k_a[j], "r_k": rw_r_k[j].reshape(-1),
                "gn_g": rw_gn_g[j], "gn_b": rw_gn_b[j],
            }
            reads_c, states = _rwkv_stream(hc, (csh1, csc1), p, None, ctx_live)
            reads, _ = _rwkv_stream(h, (sh1, sc1), p, states, True)
            w_o = bf(rw_w_o[j])
            h_new = _mm(reads[0], w_o, pro="sum_gate", extra=reads[1], epi="postnorm",
                        post=(h, ga1, g1n, b1n), name="rw_out")
            hc_new = None
            if ctx_live:
                hc_new = _mm(reads_c[0], w_o, pro="sum_gate", extra=reads_c[1], epi="postnorm",
                             post=(hc, cga1, g1n, b1n), name="rw_out_ctx")

        wr = jnp.pad(moe_router[i], ((0, 0), (0, LANES - E)))
        h = _moe_block(h_new, (sh2, sc2), ga2, g2n, b2n, wr, wg, wu, wd, i)
        if ctx_live:
            hc = _moe_block(hc_new, (csh2, csc2), cga2, g2n, b2n, wr, wg, wu, wd, i)
    return h
```

```python
import functools

import numpy as np
import jax
import jax.numpy as jnp
from jax import lax
from jax.experimental import pallas as pl
from jax.experimental.pallas import tpu as pltpu

DEPTH = 4
N_MIXERS = 3
GRID_W = 64
CONV_WIDTH = 31
NA_HEADS = 16
NA_WIN_ROWS = 8
NA_WIN_COLS = 16
RWKV_HEAD_DIM = 64
N_EXPERTS = 16
EC_CAPACITY = 2
DN_ALPHA = (2 * DEPTH) ** 0.25
LN_EPS = 1e-5
GN_EPS = 64e-5

LANES = 128
SUBLANES = 8
VMEM_LIMIT = 56 * 1024 * 1024

NEG_BIG = -1e30

_f32 = jnp.float32
_bf16 = jnp.bfloat16


def _cparams(sem):
    return pltpu.CompilerParams(dimension_semantics=sem, vmem_limit_bytes=VMEM_LIMIT)


def _sigmoid(x):
    return 1.0 / (1.0 + jnp.exp(-x))


def _silu(x):
    return x * _sigmoid(x)


def _ln_rows(x, g, b, eps):
    mu = jnp.mean(x, axis=-1, keepdims=True)
    xc = x - mu
    var = jnp.mean(xc * xc, axis=-1, keepdims=True)
    return xc * lax.rsqrt(var + eps) * g + b


def _dot(a, b):
    return jnp.dot(a, b, preferred_element_type=_f32)


def _dot_nt(a, b):
    return lax.dot_general(a, b, (((1,), (1,)), ((), ())), preferred_element_type=_f32)


def _ada_kernel(c_ref, w_ref, b_ref, o_ref):
    s = _silu(c_ref[...]).astype(_bf16)
    o_ref[0] = _dot(s, w_ref[0]) + b_ref[0]


def _ada_all(cond, ada_w, ada_b):
    R, D = cond.shape
    L, _, N = ada_w.shape
    tn = D
    return pl.pallas_call(
        _ada_kernel,
        grid=(L, N // tn),
        in_specs=[pl.BlockSpec((R, D), lambda l, n: (0, 0)),
                  pl.BlockSpec((1, D, tn), lambda l, n: (l, 0, n)),
                  pl.BlockSpec((1, 1, tn), lambda l, n: (l, 0, n))],
        out_specs=pl.BlockSpec((1, R, tn), lambda l, n: (l, 0, n)),
        out_shape=jax.ShapeDtypeStruct((L, R, N), _f32),
        compiler_params=_cparams(("arbitrary", "arbitrary")),
        name="ada",
    )(cond, ada_w, ada_b)


def _mm_kernel(*refs, pro, epi, has_bias, tm):
    it = iter(refs)
    x_ref = next(it)
    x = x_ref[0]
    if pro in ("mod", "mod_lerp"):
        sh_ref, sc_ref = next(it), next(it)
        sh, sc1 = sh_ref[0], 1.0 + sc_ref[0]
        x = x * sc1 + sh
    if pro == "mod_lerp":
        hp_ref, hn_ref, mup_ref, mun_ref = next(it), next(it), next(it), next(it)
        t = pl.program_id(2)
        nt = pl.num_programs(2)
        prev_row = jnp.where(t == 0, 0.0, hp_ref[0, 0] * sc1 + sh)
        next_row = jnp.where(t == nt - 1, 0.0, hn_ref[0, 0] * sc1 + sh)
        rows = lax.broadcasted_iota(jnp.int32, x.shape, 0)
        x_prev = jnp.where(rows == 0, prev_row, pltpu.roll(x, 1, 0))
        x_next = jnp.where(rows == tm - 1, next_row, pltpu.roll(x, tm - 1, 0))
        x = x + (x_prev - x) * mup_ref[...] + (x_next - x) * mun_ref[...]
    if pro == "ln_silu":
        g_ref, b_ref = next(it), next(it)
        x = _silu(_ln_rows(x, g_ref[...], b_ref[...], LN_EPS))
    if pro == "sum_gate":
        x2_ref, xg_ref = next(it), next(it)
        x = (x + x2_ref[0]) * xg_ref[0]
    xb = x.astype(_bf16)
    w_ref = next(it)
    y = _dot(xb, w_ref[...])
    if epi == "glu":
        w2_ref = next(it)
        y2 = _dot(xb, w2_ref[...])
    if has_bias:
        y = y + next(it)[...]
        if epi == "glu":
            y2 = y2 + next(it)[...]
    if epi == "glu":
        y = y * _sigmoid(y2)
    if epi == "postnorm":
        h_ref, ga_ref, g_ref, b_ref = next(it), next(it), next(it), next(it)
        y = _ln_rows(DN_ALPHA * h_ref[0] + ga_ref[0] * y, g_ref[...], b_ref[...], LN_EPS)
    o_ref = next(it)
    o_ref[0] = y.astype(o_ref.dtype)


def _row_tile(T):
    return 512 if T % 512 == 0 else T


def _mm(x, w, *, pro=None, epi=None, bias=None, mod=None, lerp=None, ln=None, extra=None, post=None,
        tn=None, out_dtype=_f32, name="mm"):
    B, T, K = x.shape[-3:]
    stacked = w.ndim == 3
    N = w.shape[-1]
    n_out = N // 2 if epi == "glu" else N
    tm = _row_tile(T)
    tn = n_out if tn is None else tn
    nt, nn = T // tm, (w.shape[0] if stacked else n_out // tn)
    grid = (nn, B, nt)
    xmap = lambda n, b, t: (b, t, 0)
    vecmap = lambda n, b, t: (b, 0, 0)
    if pro == "sum_gate":
        args = [x, x]
        specs = [pl.BlockSpec((None, 1, tm, K), lambda n, b, t: (0, b, t, 0)),
                 pl.BlockSpec((None, 1, tm, K), lambda n, b, t: (1, b, t, 0))]
    else:
        args, specs = [x], [pl.BlockSpec((1, tm, K), xmap)]
    if pro in ("mod", "mod_lerp"):
        shift, scale = mod
        args += [shift, scale]
        specs += [pl.BlockSpec((1, 1, K), vecmap)] * 2
    if pro == "mod_lerp":
        mu_prev, mu_next = lerp
        zero = jnp.zeros((B, 1, K), x.dtype)
        hp = jnp.concatenate([zero, x[:, tm - 1:T - 1:tm]], axis=1)[:, :, None]
        hn = jnp.concatenate([x[:, tm::tm], zero], axis=1)[:, :, None]
        args += [hp, hn, mu_prev, mu_next]
        specs += [pl.BlockSpec((1, 1, 1, K), lambda n, b, t: (b, t, 0, 0))] * 2
        if stacked:
            specs += [pl.BlockSpec((None, 1, K), lambda n, b, t: (n, 0, 0))] * 2
        else:
            specs += [pl.BlockSpec((1, K), lambda n, b, t: (0, 0))] * 2
    if pro == "ln_silu":
        args += list(ln)
        specs += [pl.BlockSpec((1, K), lambda n, b, t: (0, 0))] * 2
    if pro == "sum_gate":
        args.append(extra)
        specs.append(pl.BlockSpec((1, tm, K), xmap))
    args.append(w)
    if stacked:
        specs.append(pl.BlockSpec((None, K, tn), lambda n, b, t: (n, 0, 0)))
    else:
        specs.append(pl.BlockSpec((K, tn), lambda n, b, t: (0, n)))
    if epi == "glu":
        args.append(w)
        specs.append(pl.BlockSpec((K, tn), lambda n, b, t: (0, nn + n)))
    if bias is not None:
        args.append(bias)
        specs.append(pl.BlockSpec((1, tn), lambda n, b, t: (0, n)))
        if epi == "glu":
            args.append(bias)
            specs.append(pl.BlockSpec((1, tn), lambda n, b, t: (0, nn + n)))
    if epi == "postnorm":
        assert tn == n_out
        h, gate, g, b = post
        args += [h, gate, g, b]
        specs += [pl.BlockSpec((1, tm, n_out), xmap), pl.BlockSpec((1, 1, n_out), vecmap),
                  pl.BlockSpec((1, n_out), lambda n, b_, t: (0, 0)), pl.BlockSpec((1, n_out), lambda n, b_, t: (0, 0))]
    kern = functools.partial(_mm_kernel, pro=pro, epi=epi, has_bias=bias is not None, tm=tm)
    if stacked:
        out_spec = pl.BlockSpec((None, 1, tm, tn), lambda n, b, t: (n, b, t, 0))
        out_shape = jax.ShapeDtypeStruct((nn, B, T, n_out), out_dtype)
    else:
        out_spec = pl.BlockSpec((1, tm, tn), lambda n, b, t: (b, t, n))
        out_shape = jax.ShapeDtypeStruct((B, T, n_out), out_dtype)
    return pl.pallas_call(
        kern, grid=grid, in_specs=specs,
        out_specs=out_spec, out_shape=out_shape,
        compiler_params=_cparams(("arbitrary", "arbitrary", "arbitrary")),
        name=name,
    )(*args)


CONV_PAD = 16
CONV_ROWS = 64


def _dwconv_kernel(x_ref, w_ref, b_ref, o_ref, pad_ref, *, T):
    half = CONV_WIDTH // 2
    zeros = jnp.zeros((CONV_PAD, LANES), _f32)
    pad_ref[pl.ds(0, CONV_PAD), :] = zeros
    pad_ref[pl.ds(CONV_PAD + T, CONV_PAD), :] = zeros
    pad_ref[pl.ds(CONV_PAD, T), :] = x_ref[0]
    w = w_ref[...]
    bias = b_ref[...]

    def chunk(i, carry):
        r0 = pl.multiple_of(i * CONV_ROWS, CONV_ROWS)
        win = pad_ref.at[pl.ds(r0, CONV_ROWS + 2 * CONV_PAD), :]
        acc = jnp.zeros((CONV_ROWS, LANES), _f32) + bias
        for j in range(CONV_WIDTH):
            acc = acc + win[pl.ds(CONV_PAD - half + j, CONV_ROWS), :] * w[j:j + 1, :]
        o_ref[0, pl.ds(r0, CONV_ROWS), :] = acc
        return carry

    lax.fori_loop(0, T // CONV_ROWS, chunk, 0)


def _dwconv(u, w_dw, b_dw):
    B, T, C = u.shape
    return pl.pallas_call(
        functools.partial(_dwconv_kernel, T=T),
        grid=(B, C // LANES),
        in_specs=[pl.BlockSpec((1, T, LANES), lambda b, c: (b, 0, c)),
                  pl.BlockSpec((CONV_WIDTH, LANES), lambda b, c: (0, c)),
                  pl.BlockSpec((1, LANES), lambda b, c: (0, c))],
        out_specs=pl.BlockSpec((1, T, LANES), lambda b, c: (b, 0, c)),
        out_shape=jax.ShapeDtypeStruct((B, T, C), _f32),
        scratch_shapes=[pltpu.VMEM((T + 2 * CONV_PAD, LANES), _f32)],
        compiler_params=_cparams(("arbitrary", "arbitrary")),
        name="dwconv",
    )(u, w_dw, b_dw)


NA_Q_ROWS = 4
NA_BAND_ROWS = NA_Q_ROWS + NA_WIN_ROWS


def _na_bias_tables(rpb, rows):
    H = rpb.shape[0]
    W = GRID_W
    padded = jnp.pad(rpb, ((0, 0), (0, 0), (W, W)))
    toep = jnp.stack([padded[:, :, W + NA_WIN_COLS - 1 - qc: 2 * W + NA_WIN_COLS - 1 - qc] for qc in range(W)], axis=2)
    out = []
    for r_s in (0, NA_Q_ROWS, rows - NA_Q_ROWS):
        start = int(np.clip(r_s - NA_WIN_ROWS // 2, 0, rows - NA_BAND_ROWS))
        q_row = r_s + np.repeat(np.arange(NA_Q_ROWS), W)
        q_col = np.tile(np.arange(W), NA_Q_ROWS)
        k_row = start + np.repeat(np.arange(NA_BAND_ROWS), W)
        k_col = np.tile(np.arange(W), NA_BAND_ROWS)
        row0 = np.clip(q_row - NA_WIN_ROWS // 2, 0, rows - NA_WIN_ROWS)
        col0 = np.clip(q_col - NA_WIN_COLS // 2, 0, W - NA_WIN_COLS)
        ok = ((k_row[None, :] >= row0[:, None]) & (k_row[None, :] < row0[:, None] + NA_WIN_ROWS)
              & (k_col[None, :] >= col0[:, None]) & (k_col[None, :] < col0[:, None] + NA_WIN_COLS))
        dr = (start + np.arange(NA_BAND_ROWS))[None, :] - (r_s + np.arange(NA_Q_ROWS))[:, None]
        ridx = np.clip(dr + NA_WIN_ROWS - 1, 0, 2 * NA_WIN_ROWS - 2)
        slabs = jnp.stack([jnp.stack([toep[:, ridx[a, b]] for b in range(NA_BAND_ROWS)], axis=2)
                           for a in range(NA_Q_ROWS)], axis=1)
        bias = slabs.reshape(H, NA_Q_ROWS * W, NA_BAND_ROWS * W)
        out.append(jnp.where(jnp.asarray(ok)[None], bias, NEG_BIG))
    return jnp.stack(out)


def _na_kernel(q_ref, k_ref, v_ref, kc_ref, vc_ref, bias_ref, o_ref, *, rows, dh):
    rt = pl.program_id(2)
    n_rt = pl.num_programs(2)
    nq = NA_Q_ROWS * GRID_W
    nk = NA_BAND_ROWS * GRID_W
    start = jnp.clip(rt * NA_Q_ROWS - NA_WIN_ROWS // 2, 0, rows - NA_BAND_ROWS) * GRID_W
    start = pl.multiple_of(start, GRID_W)
    variant = jnp.where(rt == 0, 0, jnp.where(rt == n_rt - 1, 2, 1))
    q = q_ref[0] * (dh ** -0.5)
    k = k_ref[0, pl.ds(start, nk), :].astype(_bf16)
    v = v_ref[0, pl.ds(start, nk), :].astype(_bf16)
    kc = kc_ref[0].astype(_bf16)
    vc = vc_ref[0].astype(_bf16)
    lane = lax.broadcasted_iota(jnp.int32, (nq, LANES), 1)
    out = jnp.zeros((nq, LANES), _f32)
    for h in range(LANES // dh):
        in_head = (lane >= h * dh) & (lane < (h + 1) * dh)
        qh = jnp.where(in_head, q, 0.0).astype(_bf16)
        s_loc = _dot_nt(qh, k) + bias_ref[variant, 0, h]
        s_ctx = _dot_nt(qh, kc)
        m = jnp.maximum(jnp.max(s_loc, axis=-1, keepdims=True), jnp.max(s_ctx, axis=-1, keepdims=True))
        p_loc = jnp.exp(s_loc - m)
        p_ctx = jnp.exp(s_ctx - m)
        denom = jnp.sum(p_loc, axis=-1, keepdims=True) + jnp.sum(p_ctx, axis=-1, keepdims=True)
        o = (_dot(p_loc.astype(_bf16), v) + _dot(p_ctx.astype(_bf16), vc)) / denom
        out = jnp.where(in_head, o, out)
    o_ref[0] = out.astype(o_ref.dtype)


def _na_attention(qkv, qkv_c, bias_tab):
    B, T, D3 = qkv.shape
    D = D3 // 3
    L = qkv_c.shape[1]
    rows = T // GRID_W
    dh = D // NA_HEADS
    ncb = D // LANES
    nq = NA_Q_ROWS * GRID_W
    nk = NA_BAND_ROWS * GRID_W
    per = LANES // dh
    return pl.pallas_call(
        functools.partial(_na_kernel, rows=rows, dh=dh),
        grid=(ncb, B, rows // NA_Q_ROWS),
        in_specs=[pl.BlockSpec((1, nq, LANES), lambda c, b, r: (b, r, c)),
                  pl.BlockSpec((1, T, LANES), lambda c, b, r: (b, 0, ncb + c)),
                  pl.BlockSpec((1, T, LANES), lambda c, b, r: (b, 0, 2 * ncb + c)),
                  pl.BlockSpec((1, L, LANES), lambda c, b, r: (b, 0, ncb + c)),
                  pl.BlockSpec((1, L, LANES), lambda c, b, r: (b, 0, 2 * ncb + c)),
                  pl.BlockSpec((3, 1, per, nq, nk), lambda c, b, r: (0, c, 0, 0, 0))],
        out_specs=pl.BlockSpec((1, nq, LANES), lambda c, b, r: (b, r, c)),
        out_shape=jax.ShapeDtypeStruct((B, T, D), _bf16),
        compiler_params=_cparams(("arbitrary", "arbitrary", "arbitrary")),
        name="na_attn",
    )(qkv, qkv, qkv, qkv_c, qkv_c, bias_tab)


def _ctx_attn_kernel(q_ref, k_ref, v_ref, o_ref, *, dh):
    L = q_ref.shape[1]
    q = q_ref[0] * (dh ** -0.5)
    k = k_ref[0].astype(_bf16)
    v = v_ref[0].astype(_bf16)
    lane = lax.broadcasted_iota(jnp.int32, (L, LANES), 1)
    out = jnp.zeros((L, LANES), _f32)
    for h in range(LANES // dh):
        in_head = (lane >= h * dh) & (lane < (h + 1) * dh)
        qh = jnp.where(in_head, q, 0.0).astype(_bf16)
        s = _dot_nt(qh, k)
        p = jnp.exp(s - jnp.max(s, axis=-1, keepdims=True))
        o = _dot(p.astype(_bf16), v) / jnp.sum(p, axis=-1, keepdims=True)
        out = jnp.where(in_head, o, out)
    o_ref[0] = out.astype(o_ref.dtype)


def _ctx_attention(qkv_c):
    B, L, D3 = qkv_c.shape
    D = D3 // 3
    ncb = D // LANES
    dh = D // NA_HEADS
    return pl.pallas_call(
        functools.partial(_ctx_attn_kernel, dh=dh),
        grid=(B, ncb),
        in_specs=[pl.BlockSpec((1, L, LANES), lambda b, c: (b, 0, c)),
                  pl.BlockSpec((1, L, LANES), lambda b, c: (b, 0, ncb + c)),
                  pl.BlockSpec((1, L, LANES), lambda b, c: (b, 0, 2 * ncb + c))],
        out_specs=pl.BlockSpec((1, L, LANES), lambda b, c: (b, 0, c)),
        out_shape=jax.ShapeDtypeStruct((B, L, D), _bf16),
        compiler_params=_cparams(("arbitrary", "arbitrary")),
        name="ctx_attn",
    )(qkv_c, qkv_c, qkv_c)


def _rw_lora_kernel(x_ref, sh_ref, sc_ref, hp_ref, hn_ref, mup_ref, mun_ref, w1_ref, a1_ref, g1_ref,
                    w2_ref, a2_ref, g2_ref, w0_ref, a0_ref, lw_ref, ia_ref, *maybe_g, tm, emit):
    sh, sc1 = sh_ref[0], 1.0 + sc_ref[0]
    x = x_ref[0] * sc1 + sh
    t = pl.program_id(1)
    nt = pl.num_programs(1)
    prev_row = jnp.where(t == 0, 0.0, hp_ref[0, 0] * sc1 + sh)
    next_row = jnp.where(t == nt - 1, 0.0, hn_ref[0, 0] * sc1 + sh)
    rows = lax.broadcasted_iota(jnp.int32, x.shape, 0)
    d_prev = jnp.where(rows == 0, prev_row, pltpu.roll(x, 1, 0)) - x
    d_next = jnp.where(rows == tm - 1, next_row, pltpu.roll(x, tm - 1, 0)) - x

    def lerp(n):
        return (x + d_prev * mup_ref[n] + d_next * mun_ref[n]).astype(_bf16)

    tw = jnp.tanh(_dot(lerp(0), w1_ref[...])).astype(_bf16)
    ta = _dot(lerp(1), a1_ref[...]).astype(_bf16)
    for d in range(2):
        z = w0_ref[d] + _dot(tw, w2_ref[d])
        softplus_neg = jnp.maximum(-z, 0.0) + jnp.log(1.0 + jnp.exp(-jnp.abs(z)))
        lw_ref[d, 0] = -jnp.exp(-softplus_neg - 0.5)
        ia_ref[d, 0] = _sigmoid(a0_ref[d] + _dot(ta, a2_ref[d]))
    if emit:
        g_ref = maybe_g[0]
        tg = _sigmoid(_dot(lerp(2), g1_ref[...])).astype(_bf16)
        g_ref[0] = _dot(tg, g2_ref[...])


def _rw_lora(h, mod, mu_prev3, mu_next3, w1c, a1c, g1, w2p, a2p, g2, w0, a0, emit):
    B, T, D = h.shape
    tm = _row_tile(T)
    nt = T // tm
    shift, scale = mod
    zero = jnp.zeros((B, 1, D), h.dtype)
    hp = jnp.concatenate([zero, h[:, tm - 1:T - 1:tm]], axis=1)[:, :, None]
    hn = jnp.concatenate([h[:, tm::tm], zero], axis=1)[:, :, None]
    R2 = w1c.shape[1]
    G = g1.shape[1]
    c2 = lambda b, t: (0, 0)
    c3 = lambda b, t: (0, 0, 0)
    out_shape = [jax.ShapeDtypeStruct((2, B, T, D), _f32), jax.ShapeDtypeStruct((2, B, T, D), _f32)]
    out_specs = [pl.BlockSpec((2, 1, tm, D), lambda b, t: (0, b, t, 0))] * 2
    if emit:
        out_shape.append(jax.ShapeDtypeStruct((B, T, D), _f32))
        out_specs.append(pl.BlockSpec((1, tm, D), lambda b, t: (b, t, 0)))
    return pl.pallas_call(
        functools.partial(_rw_lora_kernel, tm=tm, emit=emit),
        grid=(B, nt),
        in_specs=[pl.BlockSpec((1, tm, D), lambda b, t: (b, t, 0)),
                  pl.BlockSpec((1, 1, D), lambda b, t: (b, 0, 0)),
                  pl.BlockSpec((1, 1, D), lambda b, t: (b, 0, 0)),
                  pl.BlockSpec((1, 1, 1, D), lambda b, t: (b, t, 0, 0)),
                  pl.BlockSpec((1, 1, 1, D), lambda b, t: (b, t, 0, 0)),
                  pl.BlockSpec((3, 1, D), c3), pl.BlockSpec((3, 1, D), c3),
                  pl.BlockSpec((D, R2), c2), pl.BlockSpec((D, R2), c2), pl.BlockSpec((D, G), c2),
                  pl.BlockSpec((2, R2, D), c3), pl.BlockSpec((2, R2, D), c3), pl.BlockSpec((G, D), c2),
                  pl.BlockSpec((2, 1, D), c3), pl.BlockSpec((2, 1, D), c3)],
        out_specs=out_specs, out_shape=out_shape,
        compiler_params=_cparams(("arbitrary", "arbitrary")),
        name="rw_lora",
    )(h, shift, scale, hp, hn, mu_prev3, mu_next3, w1c, a1c, g1, w2p, a2p, g2, w0, a0)


RW_TC = 32


def _rw_scan_kernel(*refs, emit, has_init, dh):
    it = iter(refs)
    lw_ref, ia_ref, k_ref, v_ref = next(it), next(it), next(it), next(it)
    r_ref = next(it) if emit else None
    kk_ref, ka_ref = next(it), next(it)
    if emit:
        rk_ref, gg_ref, gb_ref = next(it), next(it), next(it)
    s0_ref = next(it) if has_init else None
    y_ref = next(it) if emit else None
    sfin_ref = next(it)
    s_ref, w_s, a_s, b_s, kd_s = next(it), next(it), next(it), next(it), next(it)

    d = pl.program_id(0)
    c = pl.program_id(2)
    nc = pl.num_programs(2)

    @pl.when(c == 0)
    def _():
        if has_init:
            s_ref[...] = s0_ref[0, 0]
        else:
            s_ref[...] = jnp.zeros(s_ref.shape, _f32)

    def step(i, carry):
        t = jnp.where(d == 0, i, RW_TC - 1 - i)
        k_t = k_ref[t]
        ia_t = ia_ref[0, t]
        v_t = v_ref[t]
        kkr = k_t * kk_ref[...]
        kk = kkr * lax.rsqrt(jnp.maximum(jnp.sum(kkr * kkr, axis=0, keepdims=True), 1e-24))
        kd = k_t * (1.0 + (ia_t - 1.0) * ka_ref[...])
        w_s[...] = jnp.exp(lw_ref[0, t])
        a_s[...] = -kk
        b_s[...] = kk * ia_t
        kd_s[...] = kd
        part = [jnp.zeros((dh, LANES), _f32), jnp.zeros((dh, LANES), _f32)]
        for kj in range(dh):
            part[kj % 2] = part[kj % 2] + s_ref[kj] * a_s[kj:kj + 1, :]
        sa = part[0] + part[1]
        part = [jnp.zeros((dh, LANES), _f32), jnp.zeros((dh, LANES), _f32)]
        for kj in range(dh):
            sk = s_ref[kj] * w_s[kj:kj + 1, :] + sa * b_s[kj:kj + 1, :] + v_t * kd_s[kj:kj + 1, :]
            s_ref[kj] = sk
            if emit:
                part[kj % 2] = part[kj % 2] + sk * r_ref[t, pl.ds(kj, 1), :]
        if emit:
            r_t = r_ref[t]
            y = part[0] + part[1]
            mu = jnp.mean(y, axis=0, keepdims=True)
            yc = y - mu
            var = jnp.mean(yc * yc, axis=0, keepdims=True)
            bonus = jnp.sum(r_t * kd * rk_ref[...], axis=0, keepdims=True) * v_t
            y_ref[0, t] = yc * lax.rsqrt(var + GN_EPS) * gg_ref[...] + gb_ref[...] + bonus
        return carry

    lax.fori_loop(0, RW_TC, step, 0)

    @pl.when(c == nc - 1)
    def _():
        sfin_ref[0, 0] = s_ref[...]


def _rw_scan(lwT, iaT, kvrT, par, s0, emit):
    _, T, dh, S = lwT.shape
    nsb = S // LANES
    nc = T // RW_TC

    def tmap(d, s, c):
        return (jnp.where(d == 0, c, nc - 1 - c), 0, s)

    def dtmap(d, s, c):
        return (d, jnp.where(d == 0, c, nc - 1 - c), 0, s)

    def tile(g):
        return pl.BlockSpec((None, RW_TC, dh, LANES), lambda d, s, c: (g,) + tmap(d, s, c))

    dtile = pl.BlockSpec((1, RW_TC, dh, LANES), dtmap)
    ptile = pl.BlockSpec((dh, LANES), lambda d, s, c: (0, s))
    stile = pl.BlockSpec((1, 1, dh, dh, LANES), lambda d, s, c: (d, s, 0, 0, 0))
    args, specs = [lwT, iaT, kvrT, kvrT], [dtile, dtile, tile(0), tile(1)]
    if emit:
        args.append(kvrT)
        specs.append(tile(2))
    args += [par["k_k"], par["k_a"]]
    specs += [ptile, ptile]
    if emit:
        args += [par["r_k"], par["gn_g"], par["gn_b"]]
        specs += [ptile] * 3
    if s0 is not None:
        args.append(s0)
        specs.append(stile)
    out_shape, out_specs = [], []
    if emit:
        out_shape.append(jax.ShapeDtypeStruct((2, T, dh, S), _f32))
        out_specs.append(dtile)
    out_shape.append(jax.ShapeDtypeStruct((2, nsb, dh, dh, LANES), _f32))
    out_specs.append(stile)
    res = pl.pallas_call(
        functools.partial(_rw_scan_kernel, emit=emit, has_init=s0 is not None, dh=dh),
        grid=(2, nsb, nc),
        in_specs=specs, out_specs=out_specs, out_shape=out_shape,
        scratch_shapes=[pltpu.VMEM((dh, dh, LANES), _f32)] + [pltpu.VMEM((dh, LANES), _f32)] * 4,
        compiler_params=_cparams(("arbitrary", "arbitrary", "arbitrary")),
        name="rw_scan_emit" if emit else "rw_scan_state",
    )(*args)
    return (res[0], res[1]) if emit else (None, res[0])


def _excl_cumsum_lanes(m):
    E, T = m.shape
    r = lax.broadcasted_iota(jnp.int32, (LANES, LANES), 0)
    c = lax.broadcasted_iota(jnp.int32, (LANES, LANES), 1)
    tri = jnp.where(r < c, 1.0, 0.0).astype(_bf16)
    out, offset = [], jnp.zeros((E, 1), _f32)
    for j in range(T // LANES):
        blk = m[:, j * LANES:(j + 1) * LANES]
        out.append(_dot(blk.astype(_bf16), tri) + offset)
        offset = offset + jnp.sum(blk, axis=1, keepdims=True)
    return jnp.concatenate(out, axis=1)


def _route_kernel(x_ref, sh_ref, sc_ref, wr_ref, xm_ref, slot_ref, gate_ref, idx_ref, lt_ref, *, tm, cap, n_exp):
    t = pl.program_id(1)
    nt = pl.num_programs(1)
    xm = x_ref[0] * (1.0 + sc_ref[0]) + sh_ref[0]
    bits = pltpu.bitcast(xm, jnp.int32)
    rnd = lax.shift_right_logical(bits + 0x7FFF + (lax.shift_right_logical(bits, 16) & 1), 16)
    half = xm.shape[1] // 2
    xm_ref[0] = lax.shift_left(rnd[:, :half], 16) | rnd[:, half:]
    logits = jnp.dot(xm, wr_ref[...], preferred_element_type=_f32, precision=lax.Precision.HIGHEST)
    lt_ref[:, pl.ds(pl.multiple_of(t * tm, tm), tm)] = logits.T

    @pl.when(t == nt - 1)
    def _():
        lg = lt_ref[0:n_exp, :]
        e = jnp.exp(lg - jnp.max(lg, axis=0, keepdims=True))
        aff = e / jnp.sum(e, axis=0, keepdims=True)
        bits = pltpu.bitcast(aff, jnp.int32)
        thr = jnp.zeros((n_exp, 1), jnp.int32)
        for bit in range(30, -1, -1):
            cand = thr | (1 << bit)
            cnt = jnp.sum(jnp.where(bits >= cand, 1.0, 0.0), axis=1, keepdims=True)
            thr = jnp.where(cnt >= cap, cand, thr)
        gt = jnp.where(bits > thr, 1.0, 0.0)
        eq = jnp.where(bits == thr, 1.0, 0.0)
        need = cap - jnp.sum(gt, axis=1, keepdims=True)
        sel = gt + eq * jnp.where(_excl_cumsum_lanes(eq) < need, 1.0, 0.0)
        pos = _excl_cumsum_lanes(sel)
        slot = jnp.where(sel > 0.0, pos, -1.0).astype(jnp.int32)
        slot_ref[0] = slot
        gate_ref[0] = sel * aff
        T = slot.shape[1]
        srow = lax.broadcasted_iota(jnp.int32, (cap, T), 0)
        tpos = lax.broadcasted_iota(jnp.int32, (SUBLANES, T), 1)
        digit = lax.broadcasted_iota(jnp.int32, (SUBLANES, T), 0)
        tv = jnp.where(digit == 0, tpos >> 8, jnp.where(digit == 1, tpos & 255, 0)).astype(_f32).astype(_bf16)
        for e in range(n_exp):
            hot = jnp.where(slot[e:e + 1, :] == srow, 1.0, 0.0).astype(_bf16)
            r = _dot_nt(tv, hot)
            idx_ref[0, e:e + 1, :] = (256.0 * r[0:1, :] + r[1:2, :]).astype(jnp.int32)


def _moe_route(h, mod, w_router_pad, cap):
    B, T, D = h.shape
    tm = _row_tile(T)
    shift, scale = mod
    E = N_EXPERTS
    return pl.pallas_call(
        functools.partial(_route_kernel, tm=tm, cap=cap, n_exp=E),
        grid=(B, T // tm),
        in_specs=[pl.BlockSpec((1, tm, D), lambda b, t: (b, t, 0)),
                  pl.BlockSpec((1, 1, D), lambda b, t: (b, 0, 0)),
                  pl.BlockSpec((1, 1, D), lambda b, t: (b, 0, 0)),
                  pl.BlockSpec((D, LANES), lambda b, t: (0, 0))],
        out_specs=[pl.BlockSpec((1, tm, D // 2), lambda b, t: (b, t, 0)),
                   pl.BlockSpec((1, E, T), lambda b, t: (b, 0, 0)),
                   pl.BlockSpec((1, E, T), lambda b, t: (b, 0, 0)),
                   pl.BlockSpec((1, E, cap), lambda b, t: (b, 0, 0))],
        out_shape=[jax.ShapeDtypeStruct((B, T, D // 2), jnp.int32),
                   jax.ShapeDtypeStruct((B, E, T), jnp.int32),
                   jax.ShapeDtypeStruct((B, E, T), _f32),
                   jax.ShapeDtypeStruct((B, E, cap), jnp.int32)],
        scratch_shapes=[pltpu.VMEM((LANES, T), _f32)],
        compiler_params=_cparams(("arbitrary", "arbitrary")),
        name="moe_route",
    )(h, shift, scale, w_router_pad)


FFN_CHUNKS = 4


def _expert_kernel(idx_ref, xm_ref, slot_ref, gate_ref, wg_ref, wu_ref, wd_ref, ys_ref, xs_ref, *, bb, cap):
    T = xm_ref.shape[1]
    srow = lax.broadcasted_iota(jnp.int32, (cap, T), 0)
    gates = []
    for i in range(bb):
        for s in range(cap):
            xs_ref[pl.ds(i * cap + s, 1), :] = xm_ref[i, pl.ds(idx_ref[i, 0, 0, s], 1), :]
        hit = slot_ref[i, 0] == srow
        gates.append(jnp.sum(jnp.where(hit, gate_ref[i, 0], 0.0), axis=1, keepdims=True))
    packed = xs_ref[...]
    xs = jnp.concatenate([pltpu.bitcast(packed & jnp.int32(-65536), _f32),
                          pltpu.bitcast(lax.shift_left(packed, 16), _f32)], axis=1).astype(_bf16)
    F = wg_ref.shape[2]
    fc = F // FFN_CHUNKS
    y = None
    for c in range(FFN_CHUNKS):
        cols = pl.ds(c * fc, fc)
        hid = (_silu(_dot(xs, wg_ref[0, :, cols])) * _dot(xs, wu_ref[0, :, cols])).astype(_bf16)
        part = _dot(hid, wd_ref[0, cols, :])
        y = part if y is None else y + part
    for i in range(bb):
        ys_ref[i, 0] = (y[i * cap:(i + 1) * cap] * gates[i]).astype(_bf16)


def _moe_experts(xm, slot, gate, idx, wg, wu, wd, layer, cap, bb):
    B, T, half = xm.shape
    _, E, D, F = wg.shape
    slot4 = slot[:, :, None, :]
    gate4 = gate[:, :, None, :]
    idx4 = idx[:, :, None, :]
    once = pl.Buffered(1)
    return pl.pallas_call(
        functools.partial(_expert_kernel, bb=bb, cap=cap),
        grid=(E, B // bb),
        in_specs=[pl.BlockSpec((bb, 1, 1, cap), lambda e, g: (g, e, 0, 0), memory_space=pltpu.SMEM),
                  pl.BlockSpec((bb, T, half), lambda e, g: (g, 0, 0)),
                  pl.BlockSpec((bb, 1, 1, T), lambda e, g: (g, e, 0, 0)),
                  pl.BlockSpec((bb, 1, 1, T), lambda e, g: (g, e, 0, 0)),
                  pl.BlockSpec((None, 1, D, F), lambda e, g: (layer, e, 0, 0), pipeline_mode=once),
                  pl.BlockSpec((None, 1, D, F), lambda e, g: (layer, e, 0, 0), pipeline_mode=once),
                  pl.BlockSpec((None, 1, F, D), lambda e, g: (layer, e, 0, 0), pipeline_mode=once)],
        out_specs=pl.BlockSpec((bb, 1, cap, D), lambda e, g: (g, e, 0, 0)),
        out_shape=jax.ShapeDtypeStruct((B, E, cap, D), _bf16),
        scratch_shapes=[pltpu.VMEM((bb * cap, half), jnp.int32)],
        compiler_params=_cparams(("arbitrary", "arbitrary")),
        name="moe_experts",
    )(idx4, xm, slot4, gate4, wg, wu, wd)


COMBINE_GROUP = 4


def _combine_kernel(slotc_ref, ys_ref, h_ref, ga_ref, g_ref, b_ref, o_ref, *, tm, cap, n_exp):
    col = lax.broadcasted_iota(jnp.int32, (tm, cap), 1)
    sc = slotc_ref[0]
    y = None
    for e0 in range(0, n_exp, COMBINE_GROUP):
        hot = [jnp.where(sc[:, e:e + 1] == col, 1.0, 0.0).astype(_bf16) for e in range(e0, e0 + COMBINE_GROUP)]
        part = _dot(jnp.concatenate(hot, axis=1), ys_ref[0, pl.ds(e0 * cap, COMBINE_GROUP * cap), :])
        y = part if y is None else y + part
    o_ref[0] = _ln_rows(DN_ALPHA * h_ref[0] + ga_ref[0] * y, g_ref[...], b_ref[...], LN_EPS)


def _moe_combine(slot_col, ys, h, gate_vec, g, b, cap):
    B, T, D = h.shape
    E = N_EXPERTS
    tm = _row_tile(T)
    return pl.pallas_call(
        functools.partial(_combine_kernel, tm=tm, cap=cap, n_exp=E),
        grid=(B, T // tm),
        in_specs=[pl.BlockSpec((1, tm, E), lambda b_, t: (b_, t, 0)),
                  pl.BlockSpec((1, E * cap, D), lambda b_, t: (b_, 0, 0)),
                  pl.BlockSpec((1, tm, D), lambda b_, t: (b_, t, 0)),
                  pl.BlockSpec((1, 1, D), lambda b_, t: (b_, 0, 0)),
                  pl.BlockSpec((1, D), lambda b_, t: (0, 0)),
                  pl.BlockSpec((1, D), lambda b_, t: (0, 0))],
        out_specs=pl.BlockSpec((1, tm, D), lambda b_, t: (b_, t, 0)),
        out_shape=jax.ShapeDtypeStruct((B, T, D), _f32),
        compiler_params=_cparams(("arbitrary", "arbitrary")),
        name="moe_combine",
    )(slot_col, ys, h, gate_vec, g, b)


def _moe_block(h, mod2, gate_vec, ln_g, ln_b, w_router_pad, wg, wu, wd, layer):
    B, T, D = h.shape
    cap = max(1, EC_CAPACITY * T // N_EXPERTS)
    xm, slot, gate, idx = _moe_route(h, mod2, w_router_pad, cap)
    bb = max(1, min(B, 512 // cap))
    ys = _moe_experts(xm, slot, gate, idx, wg, wu, wd, layer, cap, bb)
    slot_col = jnp.swapaxes(slot, 1, 2)
    return _moe_combine(slot_col, ys.reshape(B, N_EXPERTS * cap, D), h, gate_vec, ln_g, ln_b, cap)


def _to_seq_major(t, dh):
    *lead, B, T, D = t.shape
    n = len(lead)
    perm = tuple(range(n)) + (n + 1, n + 3, n, n + 2)
    return jnp.transpose(t.reshape(*lead, B, T, D // dh, dh), perm).reshape(*lead, T, dh, B * (D // dh))


def _from_seq_major(t, B):
    *lead, T, dh, S = t.shape
    n = len(lead)
    perm = tuple(range(n)) + (n + 2, n, n + 3, n + 1)
    return jnp.transpose(t.reshape(*lead, T, dh, B, S // B), perm).reshape(*lead, B, T, (S // B) * dh)


def _head_param(p, B, dh):
    return jnp.tile(p.reshape(-1, dh).T, (1, B))


def _rwkv_stream(h, mod, p, s0, emit):
    B, T, D = h.shape
    dh = RWKV_HEAD_DIM
    outs = _rw_lora(h, mod, p["mu_prev3"], p["mu_next3"], p["w1c"], p["a1c"], p["g1"], p["w2p"], p["a2p"], p["g2"],
                    p["w0"], p["a0"], emit)
    lw, ia = outs[0], outs[1]
    n_proj = 3 if emit else 2
    kvr = _mm(h, p["w_kvr"][:n_proj], pro="mod_lerp", mod=mod,
              lerp=(p["mu_prev_kvr"][:n_proj], p["mu_next_kvr"][:n_proj]), name="rw_kvr")
    par = {n: _head_param(p[n], B, dh) for n in ("k_k", "k_a", "r_k", "gn_g", "gn_b")}
    yT, states = _rw_scan(_to_seq_major(lw, dh), _to_seq_major(ia, dh), _to_seq_major(kvr, dh), par, s0, emit)
    if not emit:
        return None, states
    return (_from_seq_major(yT, B), outs[2]), states


def kernel(x, c, ctx, c_ctx, ada_w, ada_b, ln_g, ln_b, conv_w_in, conv_b_in, conv_w_dw, conv_b_dw, conv_ln_g, conv_ln_b, conv_w_out, conv_b_out, na_w_qkv, na_w_o, na_rpb, rw_mu_prev, rw_mu_next, rw_w_r, rw_w_k, rw_w_v, rw_w0, rw_w1, rw_w2, rw_a0, rw_a1, rw_a2, rw_k_k, rw_k_a, rw_r_k, rw_g1, rw_g2, rw_gn_g, rw_gn_b, rw_w_o, moe_router, moe_w_gate, moe_w_up, moe_w_down):
    B, T, D = x.shape
    L = ctx.shape[1]
    depth = ada_w.shape[0]
    E = moe_router.shape[-1]
    bf = lambda a: a.astype(_bf16)
    row = lambda a: a.reshape(1, -1)

    ctx_layers = [i for i in range(depth) if i % N_MIXERS != 0]
    last_ctx = max(ctx_layers) if ctx_layers else -1

    pad_rows = (-(B + 1)) % SUBLANES
    cond = jnp.concatenate([c, c_ctx[None, :], jnp.zeros((pad_rows, D), c.dtype)], axis=0)
    ada = _ada_all(cond, bf(ada_w), ada_b[:, None, :])

    def mods(i, is_ctx):
        m = ada[i, B:B + 1] if is_ctx else ada[i, :B]
        m = jnp.broadcast_to(m[:, None, :], (B, 1, 6 * D))
        return [m[:, :, k * D:(k + 1) * D] for k in range(6)]

    wg, wu, wd = bf(moe_w_gate), bf(moe_w_up), bf(moe_w_down)
    h, hc = x, ctx
    for i in range(depth):
        kind, j = i % N_MIXERS, i // N_MIXERS
        ctx_read = i <= last_ctx
        ctx_live = i < last_ctx
        sh1, sc1, ga1, sh2, sc2, ga2 = mods(i, False)
        if ctx_read:
            csh1, csc1, cga1, csh2, csc2, cga2 = mods(i, True)
        g1n, b1n = row(ln_g[i, 0]), row(ln_b[i, 0])
        g2n, b2n = row(ln_g[i, 1]), row(ln_b[i, 1])

        if kind == 0:
            w_in, w_out = bf(conv_w_in[j]), bf(conv_w_out[j])

            def conv_stream(hh, shift, scale, gate):
                u = _mm(hh, w_in, pro="mod", mod=(shift, scale), epi="glu", bias=row(conv_b_in[j]), name="conv_in")
                u = _dwconv(u, conv_w_dw[j], row(conv_b_dw[j]))
                return _mm(u, w_out, pro="ln_silu", ln=(row(conv_ln_g[j]), row(conv_ln_b[j])), bias=row(conv_b_out[j]),
                           epi="postnorm", post=(hh, gate, g1n, b1n), name="conv_out")

            h_new = conv_stream(h, sh1, sc1, ga1)
            hc_new = conv_stream(hc, csh1, csc1, cga1) if ctx_live else None
        elif kind == 1:
            w_qkv, w_o = bf(na_w_qkv[j]), bf(na_w_o[j])
            qkv = _mm(h, w_qkv, pro="mod", mod=(sh1, sc1), tn=D, out_dtype=_bf16, name="na_qkv")
            qkv_c = _mm(hc, w_qkv, pro="mod", mod=(csh1, csc1), tn=D, out_dtype=_bf16, name="na_qkv_ctx")
            tab = _na_bias_tables(na_rpb[j], T // GRID_W)
            per = LANES // (D // NA_HEADS)
            tab = tab.reshape(3, NA_HEADS // per, per, tab.shape[2], tab.shape[3])
            att = _na_attention(qkv, qkv_c, tab)
            h_new = _mm(att, w_o, epi="postnorm", post=(h, ga1, g1n, b1n), name="na_out")
            hc_new = None
            if ctx_live:
                att_c = _ctx_attention(qkv_c)
                hc_new = _mm(att_c, w_o, epi="postnorm", post=(hc, cga1, g1n, b1n), name="na_out_ctx")
        else:
            R = rw_w1.shape[-1]
            zw = jnp.zeros((R, D), _f32)
            p = {
                "mu_prev3": rw_mu_prev[j][jnp.array([1, 4, 5])][:, None, :],
                "mu_next3": rw_mu_next[j][jnp.array([1, 4, 5])][:, None, :],
                "mu_prev_kvr": rw_mu_prev[j][jnp.array([2, 3, 0])][:, None, :],
                "mu_next_kvr": rw_mu_next[j][jnp.array([2, 3, 0])][:, None, :],
                "w_kvr": bf(jnp.stack([rw_w_k[j], rw_w_v[j], rw_w_r[j]])),
                "w1c": bf(jnp.concatenate([rw_w1[j, 0], rw_w1[j, 1]], axis=1)),
                "a1c": bf(jnp.concatenate([rw_a1[j, 0], rw_a1[j, 1]], axis=1)),
                "g1": bf(rw_g1[j]), "g2": bf(rw_g2[j]),
                "w2p": bf(jnp.stack([jnp.concatenate([rw_w2[j, 0], zw]), jnp.concatenate([zw, rw_w2[j, 1]])])),
                "a2p": bf(jnp.stack([jnp.concatenate([rw_a2[j, 0], zw]), jnp.concatenate([zw, rw_a2[j, 1]])])),
                "w0": rw_w0[j][:, None, :], "a0": rw_a0[j][:, None, :],
                "k_k": rw_k_k[j], "k_a": rw_k_a[j], "r_k": rw_r_k[j].reshape(-1),
                "gn_g": rw_gn_g[j], "gn_b": rw_gn_b[j],
            }
            reads_c, states = _rwkv_stream(hc, (csh1, csc1), p, None, ctx_live)
            reads, _ = _rwkv_stream(h, (sh1, sc1), p, states, True)
            w_o = bf(rw_w_o[j])
            h_new = _mm(reads[0], w_o, pro="sum_gate", extra=reads[1], epi="postnorm",
                        post=(h, ga1, g1n, b1n), name="rw_out")
            hc_new = None
            if ctx_live:
                hc_new = _mm(reads_c[0], w_o, pro="sum_gate", extra=reads_c[1], epi="postnorm",
                             post=(hc, cga1, g1n, b1n), name="rw_out_ctx")

        wr = jnp.pad(moe_router[i], ((0, 0), (0, LANES - E)))
        h = _moe_block(h_new, (sh2, sc2), ga2, g2n, b2n, wr, wg, wu, wd, i)
        if ctx_live:
            hc = _moe_block(hc_new, (csh2, csc2), cga2, g2n, b2n, wr, wg, wu, wd, i)
    return h
```

```python
import functools

import numpy as np
import jax
import jax.numpy as jnp
from jax import lax
from jax.experimental import pallas as pl
from jax.experimental.pallas import tpu as pltpu

DEPTH = 4
N_MIXERS = 3
GRID_W = 64
CONV_WIDTH = 31
NA_HEADS = 16
NA_WIN_ROWS = 8
NA_WIN_COLS = 16
RWKV_HEAD_DIM = 64
N_EXPERTS = 16
EC_CAPACITY = 2
DN_ALPHA = (2 * DEPTH) ** 0.25
LN_EPS = 1e-5
GN_EPS = 64e-5

LANES = 128
SUBLANES = 8
VMEM_LIMIT = 56 * 1024 * 1024

NEG_BIG = -1e30

_f32 = jnp.float32
_bf16 = jnp.bfloat16


def _cparams(sem):
    return pltpu.CompilerParams(dimension_semantics=sem, vmem_limit_bytes=VMEM_LIMIT)


def _sigmoid(x):
    return 1.0 / (1.0 + jnp.exp(-x))


def _silu(x):
    return x * _sigmoid(x)


def _ln_rows(x, g, b, eps):
    mu = jnp.mean(x, axis=-1, keepdims=True)
    xc = x - mu
    var = jnp.mean(xc * xc, axis=-1, keepdims=True)
    return xc * lax.rsqrt(var + eps) * g + b


def _dot(a, b):
    return jnp.dot(a, b, preferred_element_type=_f32)


def _dot_nt(a, b):
    return lax.dot_general(a, b, (((1,), (1,)), ((), ())), preferred_element_type=_f32)


def _ada_kernel(c_ref, w_ref, b_ref, o_ref):
    s = _silu(c_ref[...]).astype(_bf16)
    o_ref[0] = _dot(s, w_ref[0]) + b_ref[0]


def _ada_all(cond, ada_w, ada_b):
    R, D = cond.shape
    L, _, N = ada_w.shape
    tn = D
    return pl.pallas_call(
        _ada_kernel,
        grid=(L, N // tn),
        in_specs=[pl.BlockSpec((R, D), lambda l, n: (0, 0)),
                  pl.BlockSpec((1, D, tn), lambda l, n: (l, 0, n)),
                  pl.BlockSpec((1, 1, tn), lambda l, n: (l, 0, n))],
        out_specs=pl.BlockSpec((1, R, tn), lambda l, n: (l, 0, n)),
        out_shape=jax.ShapeDtypeStruct((L, R, N), _f32),
        compiler_params=_cparams(("arbitrary", "arbitrary")),
        name="ada",
    )(cond, ada_w, ada_b)


def _mm_kernel(*refs, pro, epi, has_bias, tm):
    it = iter(refs)
    x_ref = next(it)
    x = x_ref[0]
    if pro in ("mod", "mod_lerp"):
        sh_ref, sc_ref = next(it), next(it)
        sh, sc1 = sh_ref[0], 1.0 + sc_ref[0]
        x = x * sc1 + sh
    if pro == "mod_lerp":
        hp_ref, hn_ref, mup_ref, mun_ref = next(it), next(it), next(it), next(it)
        t = pl.program_id(2)
        nt = pl.num_programs(2)
        prev_row = jnp.where(t == 0, 0.0, hp_ref[0, 0] * sc1 + sh)
        next_row = jnp.where(t == nt - 1, 0.0, hn_ref[0, 0] * sc1 + sh)
        rows = lax.broadcasted_iota(jnp.int32, x.shape, 0)
        x_prev = jnp.where(rows == 0, prev_row, pltpu.roll(x, 1, 0))
        x_next = jnp.where(rows == tm - 1, next_row, pltpu.roll(x, tm - 1, 0))
        x = x + (x_prev - x) * mup_ref[...] + (x_next - x) * mun_ref[...]
    if pro == "ln_silu":
        g_ref, b_ref = next(it), next(it)
        x = _silu(_ln_rows(x, g_ref[...], b_ref[...], LN_EPS))
    if pro == "sum_gate":
        x2_ref, xg_ref = next(it), next(it)
        x = (x + x2_ref[0]) * xg_ref[0]
    xb = x.astype(_bf16)
    w_ref = next(it)
    y = _dot(xb, w_ref[...])
    if epi == "glu":
        w2_ref = next(it)
        y2 = _dot(xb, w2_ref[...])
    if has_bias:
        y = y + next(it)[...]
        if epi == "glu":
            y2 = y2 + next(it)[...]
    if epi == "glu":
        y = y * _sigmoid(y2)
    if epi == "postnorm":
        h_ref, ga_ref, g_ref, b_ref = next(it), next(it), next(it), next(it)
        y = _ln_rows(DN_ALPHA * h_ref[0] + ga_ref[0] * y, g_ref[...], b_ref[...], LN_EPS)
    o_ref = next(it)
    o_ref[0] = y.astype(o_ref.dtype)


def _row_tile(T):
    return 512 if T % 512 == 0 else T


def _mm(x, w, *, pro=None, epi=None, bias=None, mod=None, lerp=None, ln=None, extra=None, post=None,
        tn=None, out_dtype=_f32, name="mm"):
    B, T, K = x.shape[-3:]
    stacked = w.ndim == 3
    N = w.shape[-1]
    n_out = N // 2 if epi == "glu" else N
    tm = _row_tile(T)
    tn = n_out if tn is None else tn
    nt, nn = T // tm, (w.shape[0] if stacked else n_out // tn)
    grid = (nn, B, nt)
    xmap = lambda n, b, t: (b, t, 0)
    vecmap = lambda n, b, t: (b, 0, 0)
    if pro == "sum_gate":
        args = [x, x]
        specs = [pl.BlockSpec((None, 1, tm, K), lambda n, b, t: (0, b, t, 0)),
                 pl.BlockSpec((None, 1, tm, K), lambda n, b, t: (1, b, t, 0))]
    else:
        args, specs = [x], [pl.BlockSpec((1, tm, K), xmap)]
    if pro in ("mod", "mod_lerp"):
        shift, scale = mod
        args += [shift, scale]
        specs += [pl.BlockSpec((1, 1, K), vecmap)] * 2
    if pro == "mod_lerp":
        mu_prev, mu_next = lerp
        zero = jnp.zeros((B, 1, K), x.dtype)
        hp = jnp.concatenate([zero, x[:, tm - 1:T - 1:tm]], axis=1)[:, :, None]
        hn = jnp.concatenate([x[:, tm::tm], zero], axis=1)[:, :, None]
        args += [hp, hn, mu_prev, mu_next]
        specs += [pl.BlockSpec((1, 1, 1, K), lambda n, b, t: (b, t, 0, 0))] * 2
        if stacked:
            specs += [pl.BlockSpec((None, 1, K), lambda n, b, t: (n, 0, 0))] * 2
        else:
            specs += [pl.BlockSpec((1, K), lambda n, b, t: (0, 0))] * 2
    if pro == "ln_silu":
        args += list(ln)
        specs += [pl.BlockSpec((1, K), lambda n, b, t: (0, 0))] * 2
    if pro == "sum_gate":
        args.append(extra)
        specs.append(pl.BlockSpec((1, tm, K), xmap))
    args.append(w)
    if stacked:
        specs.append(pl.BlockSpec((None, K, tn), lambda n, b, t: (n, 0, 0)))
    else:
        specs.append(pl.BlockSpec((K, tn), lambda n, b, t: (0, n)))
    if epi == "glu":
        args.append(w)
        specs.append(pl.BlockSpec((K, tn), lambda n, b, t: (0, nn + n)))
    if bias is not None:
        args.append(bias)
        specs.append(pl.BlockSpec((1, tn), lambda n, b, t: (0, n)))
        if epi == "glu":
            args.append(bias)
            specs.append(pl.BlockSpec((1, tn), lambda n, b, t: (0, nn + n)))
    if epi == "postnorm":
        assert tn == n_out
        h, gate, g, b = post
        args += [h, gate, g, b]
        specs += [pl.BlockSpec((1, tm, n_out), xmap), pl.BlockSpec((1, 1, n_out), vecmap),
                  pl.BlockSpec((1, n_out), lambda n, b_, t: (0, 0)), pl.BlockSpec((1, n_out), lambda n, b_, t: (0, 0))]
    kern = functools.partial(_mm_kernel, pro=pro, epi=epi, has_bias=bias is not None, tm=tm)
    if stacked:
        out_spec = pl.BlockSpec((None, 1, tm, tn), lambda n, b, t: (n, b, t, 0))
        out_shape = jax.ShapeDtypeStruct((nn, B, T, n_out), out_dtype)
    else:
        out_spec = pl.BlockSpec((1, tm, tn), lambda n, b, t: (b, t, n))
        out_shape = jax.ShapeDtypeStruct((B, T, n_out), out_dtype)
    return pl.pallas_call(
        kern, grid=grid, in_specs=specs,
        out_specs=out_spec, out_shape=out_shape,
        compiler_params=_cparams(("arbitrary", "arbitrary", "arbitrary")),
        name=name,
    )(*args)


CONV_PAD = 16
CONV_ROWS = 64


def _dwconv_kernel(x_ref, w_ref, b_ref, o_ref, pad_ref, *, T):
    half = CONV_WIDTH // 2
    zeros = jnp.zeros((CONV_PAD, LANES), _f32)
    pad_ref[pl.ds(0, CONV_PAD), :] = zeros
    pad_ref[pl.ds(CONV_PAD + T, CONV_PAD), :] = zeros
    pad_ref[pl.ds(CONV_PAD, T), :] = x_ref[0]
    w = w_ref[...]
    bias = b_ref[...]

    def chunk(i, carry):
        r0 = pl.multiple_of(i * CONV_ROWS, CONV_ROWS)
        win = pad_ref.at[pl.ds(r0, CONV_ROWS + 2 * CONV_PAD), :]
        acc = jnp.zeros((CONV_ROWS, LANES), _f32) + bias
        for j in range(CONV_WIDTH):
            acc = acc + win[pl.ds(CONV_PAD - half + j, CONV_ROWS), :] * w[j:j + 1, :]
        o_ref[0, pl.ds(r0, CONV_ROWS), :] = acc
        return carry

    lax.fori_loop(0, T // CONV_ROWS, chunk, 0)


def _dwconv(u, w_dw, b_dw):
    B, T, C = u.shape
    return pl.pallas_call(
        functools.partial(_dwconv_kernel, T=T),
        grid=(B, C // LANES),
        in_specs=[pl.BlockSpec((1, T, LANES), lambda b, c: (b, 0, c)),
                  pl.BlockSpec((CONV_WIDTH, LANES), lambda b, c: (0, c)),
                  pl.BlockSpec((1, LANES), lambda b, c: (0, c))],
        out_specs=pl.BlockSpec((1, T, LANES), lambda b, c: (b, 0, c)),
        out_shape=jax.ShapeDtypeStruct((B, T, C), _f32),
        scratch_shapes=[pltpu.VMEM((T + 2 * CONV_PAD, LANES), _f32)],
        compiler_params=_cparams(("arbitrary", "arbitrary")),
        name="dwconv",
    )(u, w_dw, b_dw)


NA_Q_ROWS = 4
NA_BAND_ROWS = NA_Q_ROWS + NA_WIN_ROWS


def _na_bias_tables(rpb, rows):
    H = rpb.shape[0]
    W = GRID_W
    padded = jnp.pad(rpb, ((0, 0), (0, 0), (W, W)))
    toep = jnp.stack([padded[:, :, W + NA_WIN_COLS - 1 - qc: 2 * W + NA_WIN_COLS - 1 - qc] for qc in range(W)], axis=2)
    out = []
    for r_s in (0, NA_Q_ROWS, rows - NA_Q_ROWS):
        start = int(np.clip(r_s - NA_WIN_ROWS // 2, 0, rows - NA_BAND_ROWS))
        q_row = r_s + np.repeat(np.arange(NA_Q_ROWS), W)
        q_col = np.tile(np.arange(W), NA_Q_ROWS)
        k_row = start + np.repeat(np.arange(NA_BAND_ROWS), W)
        k_col = np.tile(np.arange(W), NA_BAND_ROWS)
        row0 = np.clip(q_row - NA_WIN_ROWS // 2, 0, rows - NA_WIN_ROWS)
        col0 = np.clip(q_col - NA_WIN_COLS // 2, 0, W - NA_WIN_COLS)
        ok = ((k_row[None, :] >= row0[:, None]) & (k_row[None, :] < row0[:, None] + NA_WIN_ROWS)
              & (k_col[None, :] >= col0[:, None]) & (k_col[None, :] < col0[:, None] + NA_WIN_COLS))
        dr = (start + np.arange(NA_BAND_ROWS))[None, :] - (r_s + np.arange(NA_Q_ROWS))[:, None]
        ridx = np.clip(dr + NA_WIN_ROWS - 1, 0, 2 * NA_WIN_ROWS - 2)
        slabs = jnp.stack([jnp.stack([toep[:, ridx[a, b]] for b in range(NA_BAND_ROWS)], axis=2)
                           for a in range(NA_Q_ROWS)], axis=1)
        bias = slabs.reshape(H, NA_Q_ROWS * W, NA_BAND_ROWS * W)
        out.append(jnp.where(jnp.asarray(ok)[None], bias, NEG_BIG))
    return jnp.stack(out)


def _na_kernel(q_ref, k_ref, v_ref, kc_ref, vc_ref, bias_ref, o_ref, *, rows, dh):
    rt = pl.program_id(2)
    n_rt = pl.num_programs(2)
    nq = NA_Q_ROWS * GRID_W
    nk = NA_BAND_ROWS * GRID_W
    start = jnp.clip(rt * NA_Q_ROWS - NA_WIN_ROWS // 2, 0, rows - NA_BAND_ROWS) * GRID_W
    start = pl.multiple_of(start, GRID_W)
    variant = jnp.where(rt == 0, 0, jnp.where(rt == n_rt - 1, 2, 1))
    q = q_ref[0] * (dh ** -0.5)
    k = k_ref[0, pl.ds(start, nk), :].astype(_bf16)
    v = v_ref[0, pl.ds(start, nk), :].astype(_bf16)
    kc = kc_ref[0].astype(_bf16)
    vc = vc_ref[0].astype(_bf16)
    lane = lax.broadcasted_iota(jnp.int32, (nq, LANES), 1)
    out = jnp.zeros((nq, LANES), _f32)
    for h in range(LANES // dh):
        in_head = (lane >= h * dh) & (lane < (h + 1) * dh)
        qh = jnp.where(in_head, q, 0.0).astype(_bf16)
        s_loc = _dot_nt(qh, k) + bias_ref[variant, 0, h]
        s_ctx = _dot_nt(qh, kc)
        m = jnp.maximum(jnp.max(s_loc, axis=-1, keepdims=True), jnp.max(s_ctx, axis=-1, keepdims=True))
        p_loc = jnp.exp(s_loc - m)
        p_ctx = jnp.exp(s_ctx - m)
        denom = jnp.sum(p_loc, axis=-1, keepdims=True) + jnp.sum(p_ctx, axis=-1, keepdims=True)
        o = (_dot(p_loc.astype(_bf16), v) + _dot(p_ctx.astype(_bf16), vc)) / denom
        out = jnp.where(in_head, o, out)
    o_ref[0] = out.astype(o_ref.dtype)


def _na_attention(qkv, qkv_c, bias_tab):
    B, T, D3 = qkv.shape
    D = D3 // 3
    L = qkv_c.shape[1]
    rows = T // GRID_W
    dh = D // NA_HEADS
    ncb = D // LANES
    nq = NA_Q_ROWS * GRID_W
    nk = NA_BAND_ROWS * GRID_W
    per = LANES // dh
    return pl.pallas_call(
        functools.partial(_na_kernel, rows=rows, dh=dh),
        grid=(ncb, B, rows // NA_Q_ROWS),
        in_specs=[pl.BlockSpec((1, nq, LANES), lambda c, b, r: (b, r, c)),
                  pl.BlockSpec((1, T, LANES), lambda c, b, r: (b, 0, ncb + c)),
                  pl.BlockSpec((1, T, LANES), lambda c, b, r: (b, 0, 2 * ncb + c)),
                  pl.BlockSpec((1, L, LANES), lambda c, b, r: (b, 0, ncb + c)),
                  pl.BlockSpec((1, L, LANES), lambda c, b, r: (b, 0, 2 * ncb + c)),
                  pl.BlockSpec((3, 1, per, nq, nk), lambda c, b, r: (0, c, 0, 0, 0))],
        out_specs=pl.BlockSpec((1, nq, LANES), lambda c, b, r: (b, r, c)),
        out_shape=jax.ShapeDtypeStruct((B, T, D), _bf16),
        compiler_params=_cparams(("arbitrary", "arbitrary", "arbitrary")),
        name="na_attn",
    )(qkv, qkv, qkv, qkv_c, qkv_c, bias_tab)


def _ctx_attn_kernel(q_ref, k_ref, v_ref, o_ref, *, dh):
    L = q_ref.shape[1]
    q = q_ref[0] * (dh ** -0.5)
    k = k_ref[0].astype(_bf16)
    v = v_ref[0].astype(_bf16)
    lane = lax.broadcasted_iota(jnp.int32, (L, LANES), 1)
    out = jnp.zeros((L, LANES), _f32)
    for h in range(LANES // dh):
        in_head = (lane >= h * dh) & (lane < (h + 1) * dh)
        qh = jnp.where(in_head, q, 0.0).astype(_bf16)
        s = _dot_nt(qh, k)
        p = jnp.exp(s - jnp.max(s, axis=-1, keepdims=True))
        o = _dot(p.astype(_bf16), v) / jnp.sum(p, axis=-1, keepdims=True)
        out = jnp.where(in_head, o, out)
    o_ref[0] = out.astype(o_ref.dtype)


def _ctx_attention(qkv_c):
    B, L, D3 = qkv_c.shape
    D = D3 // 3
    ncb = D // LANES
    dh = D // NA_HEADS
    return pl.pallas_call(
        functools.partial(_ctx_attn_kernel, dh=dh),
        grid=(B, ncb),
        in_specs=[pl.BlockSpec((1, L, LANES), lambda b, c: (b, 0, c)),
                  pl.BlockSpec((1, L, LANES), lambda b, c: (b, 0, ncb + c)),
                  pl.BlockSpec((1, L, LANES), lambda b, c: (b, 0, 2 * ncb + c))],
        out_specs=pl.BlockSpec((1, L, LANES), lambda b, c: (b, 0, c)),
        out_shape=jax.ShapeDtypeStruct((B, L, D), _bf16),
        compiler_params=_cparams(("arbitrary", "arbitrary")),
        name="ctx_attn",
    )(qkv_c, qkv_c, qkv_c)


def _rw_lora_kernel(x_ref, sh_ref, sc_ref, hp_ref, hn_ref, mup_ref, mun_ref, w1_ref, a1_ref, g1_ref,
                    w2_ref, a2_ref, g2_ref, w0_ref, a0_ref, lw_ref, ia_ref, *maybe_g, tm, emit):
    sh, sc1 = sh_ref[0], 1.0 + sc_ref[0]
    x = x_ref[0] * sc1 + sh
    t = pl.program_id(1)
    nt = pl.num_programs(1)
    prev_row = jnp.where(t == 0, 0.0, hp_ref[0, 0] * sc1 + sh)
    next_row = jnp.where(t == nt - 1, 0.0, hn_ref[0, 0] * sc1 + sh)
    rows = lax.broadcasted_iota(jnp.int32, x.shape, 0)
    d_prev = jnp.where(rows == 0, prev_row, pltpu.roll(x, 1, 0)) - x
    d_next = jnp.where(rows == tm - 1, next_row, pltpu.roll(x, tm - 1, 0)) - x

    def lerp(n):
        return (x + d_prev * mup_ref[n] + d_next * mun_ref[n]).astype(_bf16)

    tw = jnp.tanh(_dot(lerp(0), w1_ref[...])).astype(_bf16)
    ta = _dot(lerp(1), a1_ref[...]).astype(_bf16)
    for d in range(2):
        z = w0_ref[d] + _dot(tw, w2_ref[d])
        softplus_neg = jnp.maximum(-z, 0.0) + jnp.log(1.0 + jnp.exp(-jnp.abs(z)))
        lw_ref[d, 0] = -jnp.exp(-softplus_neg - 0.5)
        ia_ref[d, 0] = _sigmoid(a0_ref[d] + _dot(ta, a2_ref[d]))
    if emit:
        g_ref = maybe_g[0]
        tg = _sigmoid(_dot(lerp(2), g1_ref[...])).astype(_bf16)
        g_ref[0] = _dot(tg, g2_ref[...])


def _rw_lora(h, mod, mu_prev3, mu_next3, w1c, a1c, g1, w2p, a2p, g2, w0, a0, emit):
    B, T, D = h.shape
    tm = _row_tile(T)
    nt = T // tm
    shift, scale = mod
    zero = jnp.zeros((B, 1, D), h.dtype)
    hp = jnp.concatenate([zero, h[:, tm - 1:T - 1:tm]], axis=1)[:, :, None]
    hn = jnp.concatenate([h[:, tm::tm], zero], axis=1)[:, :, None]
    R2 = w1c.shape[1]
    G = g1.shape[1]
    c2 = lambda b, t: (0, 0)
    c3 = lambda b, t: (0, 0, 0)
    out_shape = [jax.ShapeDtypeStruct((2, B, T, D), _f32), jax.ShapeDtypeStruct((2, B, T, D), _f32)]
    out_specs = [pl.BlockSpec((2, 1, tm, D), lambda b, t: (0, b, t, 0))] * 2
    if emit:
        out_shape.append(jax.ShapeDtypeStruct((B, T, D), _f32))
        out_specs.append(pl.BlockSpec((1, tm, D), lambda b, t: (b, t, 0)))
    return pl.pallas_call(
        functools.partial(_rw_lora_kernel, tm=tm, emit=emit),
        grid=(B, nt),
        in_specs=[pl.BlockSpec((1, tm, D), lambda b, t: (b, t, 0)),
                  pl.BlockSpec((1, 1, D), lambda b, t: (b, 0, 0)),
                  pl.BlockSpec((1, 1, D), lambda b, t: (b, 0, 0)),
                  pl.BlockSpec((1, 1, 1, D), lambda b, t: (b, t, 0, 0)),
                  pl.BlockSpec((1, 1, 1, D), lambda b, t: (b, t, 0, 0)),
                  pl.BlockSpec((3, 1, D), c3), pl.BlockSpec((3, 1, D), c3),
                  pl.BlockSpec((D, R2), c2), pl.BlockSpec((D, R2), c2), pl.BlockSpec((D, G), c2),
                  pl.BlockSpec((2, R2, D), c3), pl.BlockSpec((2, R2, D), c3), pl.BlockSpec((G, D), c2),
                  pl.BlockSpec((2, 1, D), c3), pl.BlockSpec((2, 1, D), c3)],
        out_specs=out_specs, out_shape=out_shape,
        compiler_params=_cparams(("arbitrary", "arbitrary")),
        name="rw_lora",
    )(h, shift, scale, hp, hn, mu_prev3, mu_next3, w1c, a1c, g1, w2p, a2p, g2, w0, a0)


RW_TC = 32


def _rw_scan_kernel(*refs, emit, has_init, dh):
    it = iter(refs)
    lw_ref, ia_ref, k_ref, v_ref = next(it), next(it), next(it), next(it)
    r_ref = next(it) if emit else None
    kk_ref, ka_ref = next(it), next(it)
    if emit:
        rk_ref, gg_ref, gb_ref = next(it), next(it), next(it)
    s0_ref = next(it) if has_init else None
    y_ref = next(it) if emit else None
    sfin_ref = next(it)
    s_ref, w_s, a_s, b_s, kd_s = next(it), next(it), next(it), next(it), next(it)

    d = pl.program_id(0)
    c = pl.program_id(2)
    nc = pl.num_programs(2)

    @pl.when(c == 0)
    def _():
        if has_init:
            s_ref[...] = s0_ref[0, 0]
        else:
            s_ref[...] = jnp.zeros(s_ref.shape, _f32)

    def step(i, carry):
        t = jnp.where(d == 0, i, RW_TC - 1 - i)
        k_t = k_ref[t]
        ia_t = ia_ref[0, t]
        v_t = v_ref[t]
        kkr = k_t * kk_ref[...]
        kk = kkr * lax.rsqrt(jnp.maximum(jnp.sum(kkr * kkr, axis=0, keepdims=True), 1e-24))
        kd = k_t * (1.0 + (ia_t - 1.0) * ka_ref[...])
        w_s[...] = jnp.exp(lw_ref[0, t])
        a_s[...] = -kk
        b_s[...] = kk * ia_t
        kd_s[...] = kd
        part = [jnp.zeros((dh, LANES), _f32), jnp.zeros((dh, LANES), _f32)]
        for kj in range(dh):
            part[kj % 2] = part[kj % 2] + s_ref[kj] * a_s[kj:kj + 1, :]
        sa = part[0] + part[1]
        part = [jnp.zeros((dh, LANES), _f32), jnp.zeros((dh, LANES), _f32)]
        for kj in range(dh):
            sk = s_ref[kj] * w_s[kj:kj + 1, :] + sa * b_s[kj:kj + 1, :] + v_t * kd_s[kj:kj + 1, :]
            s_ref[kj] = sk
            if emit:
                part[kj % 2] = part[kj % 2] + sk * r_ref[t, pl.ds(kj, 1), :]
        if emit:
            r_t = r_ref[t]
            y = part[0] + part[1]
            mu = jnp.mean(y, axis=0, keepdims=True)
            yc = y - mu
            var = jnp.mean(yc * yc, axis=0, keepdims=True)
            bonus = jnp.sum(r_t * kd * rk_ref[...], axis=0, keepdims=True) * v_t
            y_ref[0, t] = yc * lax.rsqrt(var + GN_EPS) * gg_ref[...] + gb_ref[...] + bonus
        return carry

    lax.fori_loop(0, RW_TC, step, 0, unroll=2)

    @pl.when(c == nc - 1)
    def _():
        sfin_ref[0, 0] = s_ref[...]


def _rw_scan(lwT, iaT, kvrT, par, s0, emit):
    _, T, dh, S = lwT.shape
    nsb = S // LANES
    nc = T // RW_TC

    def tmap(d, s, c):
        return (jnp.where(d == 0, c, nc - 1 - c), 0, s)

    def dtmap(d, s, c):
        return (d, jnp.where(d == 0, c, nc - 1 - c), 0, s)

    def tile(g):
        return pl.BlockSpec((None, RW_TC, dh, LANES), lambda d, s, c: (g,) + tmap(d, s, c))

    dtile = pl.BlockSpec((1, RW_TC, dh, LANES), dtmap)
    ptile = pl.BlockSpec((dh, LANES), lambda d, s, c: (0, s))
    stile = pl.BlockSpec((1, 1, dh, dh, LANES), lambda d, s, c: (d, s, 0, 0, 0))
    args, specs = [lwT, iaT, kvrT, kvrT], [dtile, dtile, tile(0), tile(1)]
    if emit:
        args.append(kvrT)
        specs.append(tile(2))
    args += [par["k_k"], par["k_a"]]
    specs += [ptile, ptile]
    if emit:
        args += [par["r_k"], par["gn_g"], par["gn_b"]]
        specs += [ptile] * 3
    if s0 is not None:
        args.append(s0)
        specs.append(stile)
    out_shape, out_specs = [], []
    if emit:
        out_shape.append(jax.ShapeDtypeStruct((2, T, dh, S), _f32))
        out_specs.append(dtile)
    out_shape.append(jax.ShapeDtypeStruct((2, nsb, dh, dh, LANES), _f32))
    out_specs.append(stile)
    res = pl.pallas_call(
        functools.partial(_rw_scan_kernel, emit=emit, has_init=s0 is not None, dh=dh),
        grid=(2, nsb, nc),
        in_specs=specs, out_specs=out_specs, out_shape=out_shape,
        scratch_shapes=[pltpu.VMEM((dh, dh, LANES), _f32)] + [pltpu.VMEM((dh, LANES), _f32)] * 4,
        compiler_params=_cparams(("arbitrary", "arbitrary", "arbitrary")),
        name="rw_scan_emit" if emit else "rw_scan_state",
    )(*args)
    return (res[0], res[1]) if emit else (None, res[0])


def _excl_cumsum_lanes(m):
    E, T = m.shape
    assert E & (E - 1) == 0
    nb = T // LANES
    R = nb * E
    r = lax.broadcasted_iota(jnp.int32, (LANES, LANES), 0)
    c = lax.broadcasted_iota(jnp.int32, (LANES, LANES), 1)
    tri = jnp.where(r < c, 1.0, 0.0).astype(_bf16)
    ones = jnp.ones((LANES, LANES), _bf16)
    stacked = jnp.concatenate([m[:, j * LANES:(j + 1) * LANES] for j in range(nb)], axis=0).astype(_bf16)
    within = _dot(stacked, tri)
    totals = _dot(stacked, ones)
    qi = lax.broadcasted_iota(jnp.int32, (R, R), 0)
    qj = lax.broadcasted_iota(jnp.int32, (R, R), 1)
    shift = E.bit_length() - 1
    earlier = ((qi & (E - 1)) == (qj & (E - 1))) & (lax.shift_right_logical(qj, shift) < lax.shift_right_logical(qi, shift))
    res = within + _dot(jnp.where(earlier, 1.0, 0.0).astype(_bf16), totals.astype(_bf16))
    return jnp.concatenate([res[j * E:(j + 1) * E, :] for j in range(nb)], axis=1)


def _route_kernel(x_ref, sh_ref, sc_ref, wr_ref, xm_ref, slot_ref, gate_ref, idx_ref, lt_ref, *, tm, cap, n_exp):
    t = pl.program_id(1)
    nt = pl.num_programs(1)
    xm = x_ref[0] * (1.0 + sc_ref[0]) + sh_ref[0]
    bits = pltpu.bitcast(xm, jnp.int32)
    rnd = lax.shift_right_logical(bits + 0x7FFF + (lax.shift_right_logical(bits, 16) & 1), 16)
    half = xm.shape[1] // 2
    xm_ref[0] = lax.shift_left(rnd[:, :half], 16) | rnd[:, half:]
    logits = jnp.dot(xm, wr_ref[...], preferred_element_type=_f32, precision=lax.Precision.HIGHEST)
    lt_ref[:, pl.ds(pl.multiple_of(t * tm, tm), tm)] = logits.T

    @pl.when(t == nt - 1)
    def _():
        lg = lt_ref[0:n_exp, :]
        e = jnp.exp(lg - jnp.max(lg, axis=0, keepdims=True))
        aff = e / jnp.sum(e, axis=0, keepdims=True)
        bits = pltpu.bitcast(aff, jnp.int32)
        thr = jnp.zeros((n_exp, 1), jnp.int32)
        for bit in range(30, -1, -1):
            cand = thr | (1 << bit)
            cnt = jnp.sum(jnp.where(bits >= cand, 1.0, 0.0), axis=1, keepdims=True)
            thr = jnp.where(cnt >= cap, cand, thr)
        gt = jnp.where(bits > thr, 1.0, 0.0)
        eq = jnp.where(bits == thr, 1.0, 0.0)
        need = cap - jnp.sum(gt, axis=1, keepdims=True)
        sel = gt + eq * jnp.where(_excl_cumsum_lanes(eq) < need, 1.0, 0.0)
        pos = _excl_cumsum_lanes(sel)
        slot = jnp.where(sel > 0.0, pos, -1.0).astype(jnp.int32)
        slot_ref[0] = slot
        gate_ref[0] = sel * aff
        T = slot.shape[1]
        srow = lax.broadcasted_iota(jnp.int32, (cap, T), 0).astype(_f32)
        cinc = pos + sel
        for e in range(n_exp):
            before = jnp.where(cinc[e:e + 1, :] <= srow, 1.0, 0.0)
            idx_ref[0, e] = jnp.sum(before, axis=1, keepdims=True).astype(jnp.int32)


def _moe_route(h, mod, w_router_pad, cap):
    B, T, D = h.shape
    tm = _row_tile(T)
    shift, scale = mod
    E = N_EXPERTS
    return pl.pallas_call(
        functools.partial(_route_kernel, tm=tm, cap=cap, n_exp=E),
        grid=(B, T // tm),
        in_specs=[pl.BlockSpec((1, tm, D), lambda b, t: (b, t, 0)),
                  pl.BlockSpec((1, 1, D), lambda b, t: (b, 0, 0)),
                  pl.BlockSpec((1, 1, D), lambda b, t: (b, 0, 0)),
                  pl.BlockSpec((D, LANES), lambda b, t: (0, 0))],
        out_specs=[pl.BlockSpec((1, tm, D // 2), lambda b, t: (b, t, 0)),
                   pl.BlockSpec((1, E, T), lambda b, t: (b, 0, 0)),
                   pl.BlockSpec((1, E, T), lambda b, t: (b, 0, 0)),
                   pl.BlockSpec((1, E, cap, 1), lambda b, t: (b, 0, 0, 0))],
        out_shape=[jax.ShapeDtypeStruct((B, T, D // 2), jnp.int32),
                   jax.ShapeDtypeStruct((B, E, T), jnp.int32),
                   jax.ShapeDtypeStruct((B, E, T), _f32),
                   jax.ShapeDtypeStruct((B, E, cap, 1), jnp.int32)],
        scratch_shapes=[pltpu.VMEM((LANES, T), _f32)],
        compiler_params=_cparams(("arbitrary", "arbitrary")),
        name="moe_route",
    )(h, shift, scale, w_router_pad)


FFN_CHUNKS = 4


def _expert_kernel(idx_ref, xm_ref, slot_ref, gate_ref, wg_ref, wu_ref, wd_ref, ys_ref, xs_ref, *, bb, cap):
    T = xm_ref.shape[1]
    srow = lax.broadcasted_iota(jnp.int32, (cap, T), 0)
    gates = []
    for i in range(bb):
        for s in range(cap):
            xs_ref[pl.ds(i * cap + s, 1), :] = xm_ref[i, pl.ds(idx_ref[i, 0, 0, s], 1), :]
        hit = slot_ref[i, 0] == srow
        gates.append(jnp.sum(jnp.where(hit, gate_ref[i, 0], 0.0), axis=1, keepdims=True))
    packed = xs_ref[...]
    xs = jnp.concatenate([pltpu.bitcast(packed & jnp.int32(-65536), _f32),
                          pltpu.bitcast(lax.shift_left(packed, 16), _f32)], axis=1).astype(_bf16)
    F = wg_ref.shape[2]
    fc = F // FFN_CHUNKS
    y = None
    for c in range(FFN_CHUNKS):
        cols = pl.ds(c * fc, fc)
        hid = (_silu(_dot(xs, wg_ref[0, :, cols])) * _dot(xs, wu_ref[0, :, cols])).astype(_bf16)
        part = _dot(hid, wd_ref[0, cols, :])
        y = part if y is None else y + part
    for i in range(bb):
        ys_ref[i, 0] = (y[i * cap:(i + 1) * cap] * gates[i]).astype(_bf16)


def _moe_experts(xm, slot, gate, idx, wg, wu, wd, layer, cap, bb):
    B, T, half = xm.shape
    _, E, D, F = wg.shape
    slot4 = slot[:, :, None, :]
    gate4 = gate[:, :, None, :]
    idx4 = idx.reshape(B, E, 1, cap)
    once = pl.Buffered(1)
    return pl.pallas_call(
        functools.partial(_expert_kernel, bb=bb, cap=cap),
        grid=(E, B // bb),
        in_specs=[pl.BlockSpec((bb, 1, 1, cap), lambda e, g: (g, e, 0, 0), memory_space=pltpu.SMEM),
                  pl.BlockSpec((bb, T, half), lambda e, g: (g, 0, 0)),
                  pl.BlockSpec((bb, 1, 1, T), lambda e, g: (g, e, 0, 0)),
                  pl.BlockSpec((bb, 1, 1, T), lambda e, g: (g, e, 0, 0)),
                  pl.BlockSpec((None, 1, D, F), lambda e, g: (layer, e, 0, 0), pipeline_mode=once),
                  pl.BlockSpec((None, 1, D, F), lambda e, g: (layer, e, 0, 0), pipeline_mode=once),
                  pl.BlockSpec((None, 1, F, D), lambda e, g: (layer, e, 0, 0), pipeline_mode=once)],
        out_specs=pl.BlockSpec((bb, 1, cap, D), lambda e, g: (g, e, 0, 0)),
        out_shape=jax.ShapeDtypeStruct((B, E, cap, D), _bf16),
        scratch_shapes=[pltpu.VMEM((bb * cap, half), jnp.int32)],
        compiler_params=_cparams(("arbitrary", "arbitrary")),
        name="moe_experts",
    )(idx4, xm, slot4, gate4, wg, wu, wd)


COMBINE_GROUP = 4


def _combine_kernel(slotc_ref, ys_ref, h_ref, ga_ref, g_ref, b_ref, o_ref, *, tm, cap, n_exp):
    col = lax.broadcasted_iota(jnp.int32, (tm, cap), 1)
    sc = slotc_ref[0]
    y = None
    for e0 in range(0, n_exp, COMBINE_GROUP):
        hot = [jnp.where(sc[:, e:e + 1] == col, 1.0, 0.0).astype(_bf16) for e in range(e0, e0 + COMBINE_GROUP)]
        part = _dot(jnp.concatenate(hot, axis=1), ys_ref[0, pl.ds(e0 * cap, COMBINE_GROUP * cap), :])
        y = part if y is None else y + part
    o_ref[0] = _ln_rows(DN_ALPHA * h_ref[0] + ga_ref[0] * y, g_ref[...], b_ref[...], LN_EPS)


def _moe_combine(slot_col, ys, h, gate_vec, g, b, cap):
    B, T, D = h.shape
    E = N_EXPERTS
    tm = _row_tile(T)
    return pl.pallas_call(
        functools.partial(_combine_kernel, tm=tm, cap=cap, n_exp=E),
        grid=(B, T // tm),
        in_specs=[pl.BlockSpec((1, tm, E), lambda b_, t: (b_, t, 0)),
                  pl.BlockSpec((1, E * cap, D), lambda b_, t: (b_, 0, 0)),
                  pl.BlockSpec((1, tm, D), lambda b_, t: (b_, t, 0)),
                  pl.BlockSpec((1, 1, D), lambda b_, t: (b_, 0, 0)),
                  pl.BlockSpec((1, D), lambda b_, t: (0, 0)),
                  pl.BlockSpec((1, D), lambda b_, t: (0, 0))],
        out_specs=pl.BlockSpec((1, tm, D), lambda b_, t: (b_, t, 0)),
        out_shape=jax.ShapeDtypeStruct((B, T, D), _f32),
        compiler_params=_cparams(("arbitrary", "arbitrary")),
        name="moe_combine",
    )(slot_col, ys, h, gate_vec, g, b)


def _moe_block(h, mod2, gate_vec, ln_g, ln_b, w_router_pad, wg, wu, wd, layer):
    B, T, D = h.shape
    cap = max(1, EC_CAPACITY * T // N_EXPERTS)
    xm, slot, gate, idx = _moe_route(h, mod2, w_router_pad, cap)
    bb = max(1, min(B, 512 // cap))
    ys = _moe_experts(xm, slot, gate, idx, wg, wu, wd, layer, cap, bb)
    slot_col = jnp.swapaxes(slot, 1, 2)
    return _moe_combine(slot_col, ys.reshape(B, N_EXPERTS * cap, D), h, gate_vec, ln_g, ln_b, cap)


def _to_seq_major(t, dh):
    *lead, B, T, D = t.shape
    n = len(lead)
    perm = tuple(range(n)) + (n + 1, n + 3, n, n + 2)
    return jnp.transpose(t.reshape(*lead, B, T, D // dh, dh), perm).reshape(*lead, T, dh, B * (D // dh))


def _from_seq_major(t, B):
    *lead, T, dh, S = t.shape
    n = len(lead)
    perm = tuple(range(n)) + (n + 2, n, n + 3, n + 1)
    return jnp.transpose(t.reshape(*lead, T, dh, B, S // B), perm).reshape(*lead, B, T, (S // B) * dh)


def _head_param(p, B, dh):
    return jnp.tile(p.reshape(-1, dh).T, (1, B))


def _rwkv_stream(h, mod, p, s0, emit):
    B, T, D = h.shape
    dh = RWKV_HEAD_DIM
    outs = _rw_lora(h, mod, p["mu_prev3"], p["mu_next3"], p["w1c"], p["a1c"], p["g1"], p["w2p"], p["a2p"], p["g2"],
                    p["w0"], p["a0"], emit)
    lw, ia = outs[0], outs[1]
    n_proj = 3 if emit else 2
    kvr = _mm(h, p["w_kvr"][:n_proj], pro="mod_lerp", mod=mod,
              lerp=(p["mu_prev_kvr"][:n_proj], p["mu_next_kvr"][:n_proj]), name="rw_kvr")
    par = {n: _head_param(p[n], B, dh) for n in ("k_k", "k_a", "r_k", "gn_g", "gn_b")}
    yT, states = _rw_scan(_to_seq_major(lw, dh), _to_seq_major(ia, dh), _to_seq_major(kvr, dh), par, s0, emit)
    if not emit:
        return None, states
    return (_from_seq_major(yT, B), outs[2]), states


def kernel(x, c, ctx, c_ctx, ada_w, ada_b, ln_g, ln_b, conv_w_in, conv_b_in, conv_w_dw, conv_b_dw, conv_ln_g, conv_ln_b, conv_w_out, conv_b_out, na_w_qkv, na_w_o, na_rpb, rw_mu_prev, rw_mu_next, rw_w_r, rw_w_k, rw_w_v, rw_w0, rw_w1, rw_w2, rw_a0, rw_a1, rw_a2, rw_k_k, rw_k_a, rw_r_k, rw_g1, rw_g2, rw_gn_g, rw_gn_b, rw_w_o, moe_router, moe_w_gate, moe_w_up, moe_w_down):
    B, T, D = x.shape
    L = ctx.shape[1]
    depth = ada_w.shape[0]
    E = moe_router.shape[-1]
    bf = lambda a: a.astype(_bf16)
    row = lambda a: a.reshape(1, -1)

    ctx_layers = [i for i in range(depth) if i % N_MIXERS != 0]
    last_ctx = max(ctx_layers) if ctx_layers else -1

    pad_rows = (-(B + 1)) % SUBLANES
    cond = jnp.concatenate([c, c_ctx[None, :], jnp.zeros((pad_rows, D), c.dtype)], axis=0)
    ada = _ada_all(cond, bf(ada_w), ada_b[:, None, :])

    def mods(i, is_ctx):
        m = ada[i, B:B + 1] if is_ctx else ada[i, :B]
        m = jnp.broadcast_to(m[:, None, :], (B, 1, 6 * D))
        return [m[:, :, k * D:(k + 1) * D] for k in range(6)]

    wg, wu, wd = bf(moe_w_gate), bf(moe_w_up), bf(moe_w_down)
    h, hc = x, ctx
    for i in range(depth):
        kind, j = i % N_MIXERS, i // N_MIXERS
        ctx_read = i <= last_ctx
        ctx_live = i < last_ctx
        sh1, sc1, ga1, sh2, sc2, ga2 = mods(i, False)
        if ctx_read:
            csh1, csc1, cga1, csh2, csc2, cga2 = mods(i, True)
        g1n, b1n = row(ln_g[i, 0]), row(ln_b[i, 0])
        g2n, b2n = row(ln_g[i, 1]), row(ln_b[i, 1])

        if kind == 0:
            w_in, w_out = bf(conv_w_in[j]), bf(conv_w_out[j])

            def conv_stream(hh, shift, scale, gate):
                u = _mm(hh, w_in, pro="mod", mod=(shift, scale), epi="glu", bias=row(conv_b_in[j]), name="conv_in")
                u = _dwconv(u, conv_w_dw[j], row(conv_b_dw[j]))
                return _mm(u, w_out, pro="ln_silu", ln=(row(conv_ln_g[j]), row(conv_ln_b[j])), bias=row(conv_b_out[j]),
                           epi="postnorm", post=(hh, gate, g1n, b1n), name="conv_out")

            h_new = conv_stream(h, sh1, sc1, ga1)
            hc_new = conv_stream(hc, csh1, csc1, cga1) if ctx_live else None
        elif kind == 1:
            w_qkv, w_o = bf(na_w_qkv[j]), bf(na_w_o[j])
            qkv = _mm(h, w_qkv, pro="mod", mod=(sh1, sc1), tn=D, out_dtype=_bf16, name="na_qkv")
            qkv_c = _mm(hc, w_qkv, pro="mod", mod=(csh1, csc1), tn=D, out_dtype=_bf16, name="na_qkv_ctx")
            tab = _na_bias_tables(na_rpb[j], T // GRID_W)
            per = LANES // (D // NA_HEADS)
            tab = tab.reshape(3, NA_HEADS // per, per, tab.shape[2], tab.shape[3])
            att = _na_attention(qkv, qkv_c, tab)
            h_new = _mm(att, w_o, epi="postnorm", post=(h, ga1, g1n, b1n), name="na_out")
            hc_new = None
            if ctx_live:
                att_c = _ctx_attention(qkv_c)
                hc_new = _mm(att_c, w_o, epi="postnorm", post=(hc, cga1, g1n, b1n), name="na_out_ctx")
        else:
            R = rw_w1.shape[-1]
            zw = jnp.zeros((R, D), _f32)
            p = {
                "mu_prev3": rw_mu_prev[j][jnp.array([1, 4, 5])][:, None, :],
                "mu_next3": rw_mu_next[j][jnp.array([1, 4, 5])][:, None, :],
                "mu_prev_kvr": rw_mu_prev[j][jnp.array([2, 3, 0])][:, None, :],
                "mu_next_kvr": rw_mu_next[j][jnp.array([2, 3, 0])][:, None, :],
                "w_kvr": bf(jnp.stack([rw_w_k[j], rw_w_v[j], rw_w_r[j]])),
                "w1c": bf(jnp.concatenate([rw_w1[j, 0], rw_w1[j, 1]], axis=1)),
                "a1c": bf(jnp.concatenate([rw_a1[j, 0], rw_a1[j, 1]], axis=1)),
                "g1": bf(rw_g1[j]), "g2": bf(rw_g2[j]),
                "w2p": bf(jnp.stack([jnp.concatenate([rw_w2[j, 0], zw]), jnp.concatenate([zw, rw_w2[j, 1]])])),
                "a2p": bf(jnp.stack([jnp.concatenate([rw_a2[j, 0], zw]), jnp.concatenate([zw, rw_a2[j, 1]])])),
                "w0": rw_w0[j][:, None, :], "a0": rw_a0[j][:, None, :],
                "k_k": rw_k_k[j], "k_a": rw_k_a[j], "r_k": rw_r_k[j].reshape(-1),
                "gn_g": rw_gn_g[j], "gn_b": rw_gn_b[j],
            }
            reads_c, states = _rwkv_stream(hc, (csh1, csc1), p, None, ctx_live)
            reads, _ = _rwkv_stream(h, (sh1, sc1), p, states, True)
            w_o = bf(rw_w_o[j])
            h_new = _mm(reads[0], w_o, pro="sum_gate", extra=reads[1], epi="postnorm",
                        post=(h, ga1, g1n, b1n), name="rw_out")
            hc_new = None
            if ctx_live:
                hc_new = _mm(reads_c[0], w_o, pro="sum_gate", extra=reads_c[1], epi="postnorm",
                             post=(hc, cga1, g1n, b1n), name="rw_out_ctx")

        wr = jnp.pad(moe_router[i], ((0, 0), (0, LANES - E)))
        h = _moe_block(h_new, (sh2, sc2), ga2, g2n, b2n, wr, wg, wu, wd, i)
        if ctx_live:
            hc = _moe_block(hc_new, (csh2, csc2), cga2, g2n, b2n, wr, wg, wu, wd, i)
    return h
```

```python
import functools

import numpy as np
import jax
import jax.numpy as jnp
from jax import lax
from jax.experimental import pallas as pl
from jax.experimental.pallas import tpu as pltpu

DEPTH = 4
N_MIXERS = 3
GRID_W = 64
CONV_WIDTH = 31
NA_HEADS = 16
NA_WIN_ROWS = 8
NA_WIN_COLS = 16
RWKV_HEAD_DIM = 64
N_EXPERTS = 16
EC_CAPACITY = 2
DN_ALPHA = (2 * DEPTH) ** 0.25
LN_EPS = 1e-5
GN_EPS = 64e-5

LANES = 128
SUBLANES = 8
VMEM_LIMIT = 56 * 1024 * 1024

NEG_BIG = -1e30

_f32 = jnp.float32
_bf16 = jnp.bfloat16


def _cparams(sem):
    return pltpu.CompilerParams(dimension_semantics=sem, vmem_limit_bytes=VMEM_LIMIT)


def _sigmoid(x):
    return 1.0 / (1.0 + jnp.exp(-x))


def _silu(x):
    return x * _sigmoid(x)


def _ln_rows(x, g, b, eps):
    mu = jnp.mean(x, axis=-1, keepdims=True)
    xc = x - mu
    var = jnp.mean(xc * xc, axis=-1, keepdims=True)
    return xc * lax.rsqrt(var + eps) * g + b


def _dot(a, b):
    return jnp.dot(a, b, preferred_element_type=_f32)


def _dot_nt(a, b):
    return lax.dot_general(a, b, (((1,), (1,)), ((), ())), preferred_element_type=_f32)


def _ada_kernel(c_ref, w_ref, b_ref, o_ref):
    s = _silu(c_ref[...]).astype(_bf16)
    o_ref[0] = _dot(s, w_ref[0]) + b_ref[0]


def _ada_all(cond, ada_w, ada_b):
    R, D = cond.shape
    L, _, N = ada_w.shape
    tn = D
    return pl.pallas_call(
        _ada_kernel,
        grid=(L, N // tn),
        in_specs=[pl.BlockSpec((R, D), lambda l, n: (0, 0)),
                  pl.BlockSpec((1, D, tn), lambda l, n: (l, 0, n)),
                  pl.BlockSpec((1, 1, tn), lambda l, n: (l, 0, n))],
        out_specs=pl.BlockSpec((1, R, tn), lambda l, n: (l, 0, n)),
        out_shape=jax.ShapeDtypeStruct((L, R, N), _f32),
        compiler_params=_cparams(("arbitrary", "arbitrary")),
        name="ada",
    )(cond, ada_w, ada_b)


def _mm_kernel(*refs, pro, epi, has_bias, tm, transposed=False):
    it = iter(refs)
    x_ref = next(it)
    x = x_ref[0]
    if pro in ("mod", "mod_lerp"):
        sh_ref, sc_ref = next(it), next(it)
        sh, sc1 = sh_ref[0], 1.0 + sc_ref[0]
        x = x * sc1 + sh
    if pro == "mod_lerp":
        hp_ref, hn_ref, mup_ref, mun_ref = next(it), next(it), next(it), next(it)
        t = pl.program_id(2)
        nt = pl.num_programs(2)
        prev_row = jnp.where(t == 0, 0.0, hp_ref[0, 0] * sc1 + sh)
        next_row = jnp.where(t == nt - 1, 0.0, hn_ref[0, 0] * sc1 + sh)
        rows = lax.broadcasted_iota(jnp.int32, x.shape, 0)
        x_prev = jnp.where(rows == 0, prev_row, pltpu.roll(x, 1, 0))
        x_next = jnp.where(rows == tm - 1, next_row, pltpu.roll(x, tm - 1, 0))
        x = x + (x_prev - x) * mup_ref[...] + (x_next - x) * mun_ref[...]
    if pro == "ln_silu":
        g_ref, b_ref = next(it), next(it)
        x = _silu(_ln_rows(x, g_ref[...], b_ref[...], LN_EPS))
    if pro == "sum_gate":
        x2_ref, xg_ref = next(it), next(it)
        x = (x + x2_ref[0]) * xg_ref[0]
    xb = x.astype(_bf16)
    w_ref = next(it)
    y = _dot_nt(w_ref[...], xb) if transposed else _dot(xb, w_ref[...])
    if epi == "glu":
        w2_ref = next(it)
        y2 = _dot(xb, w2_ref[...])
    if has_bias:
        y = y + next(it)[...]
        if epi == "glu":
            y2 = y2 + next(it)[...]
    if epi == "glu":
        y = y * _sigmoid(y2)
    if epi == "postnorm":
        h_ref, ga_ref, g_ref, b_ref = next(it), next(it), next(it), next(it)
        y = _ln_rows(DN_ALPHA * h_ref[0] + ga_ref[0] * y, g_ref[...], b_ref[...], LN_EPS)
    o_ref = next(it)
    o_ref[0] = y.astype(o_ref.dtype)


def _row_tile(T):
    return 512 if T % 512 == 0 else T


def _mm(x, w, *, pro=None, epi=None, bias=None, mod=None, lerp=None, ln=None, extra=None, post=None,
        tn=None, out_dtype=_f32, name="mm"):
    B, T, K = x.shape[-3:]
    stacked = w.ndim == 3
    N = w.shape[1] if stacked else w.shape[-1]
    n_out = N // 2 if epi == "glu" else N
    tm = _row_tile(T)
    tn = n_out if tn is None else tn
    nt, nn = T // tm, (w.shape[0] if stacked else n_out // tn)
    grid = (nn, B, nt)
    xmap = lambda n, b, t: (b, t, 0)
    vecmap = lambda n, b, t: (b, 0, 0)
    if pro == "sum_gate":
        args = [x, x]
        specs = [pl.BlockSpec((None, 1, tm, K), lambda n, b, t: (0, b, t, 0)),
                 pl.BlockSpec((None, 1, tm, K), lambda n, b, t: (1, b, t, 0))]
    else:
        args, specs = [x], [pl.BlockSpec((1, tm, K), xmap)]
    if pro in ("mod", "mod_lerp"):
        shift, scale = mod
        args += [shift, scale]
        specs += [pl.BlockSpec((1, 1, K), vecmap)] * 2
    if pro == "mod_lerp":
        mu_prev, mu_next = lerp
        zero = jnp.zeros((B, 1, K), x.dtype)
        hp = jnp.concatenate([zero, x[:, tm - 1:T - 1:tm]], axis=1)[:, :, None]
        hn = jnp.concatenate([x[:, tm::tm], zero], axis=1)[:, :, None]
        args += [hp, hn, mu_prev, mu_next]
        specs += [pl.BlockSpec((1, 1, 1, K), lambda n, b, t: (b, t, 0, 0))] * 2
        if stacked:
            specs += [pl.BlockSpec((None, 1, K), lambda n, b, t: (n, 0, 0))] * 2
        else:
            specs += [pl.BlockSpec((1, K), lambda n, b, t: (0, 0))] * 2
    if pro == "ln_silu":
        args += list(ln)
        specs += [pl.BlockSpec((1, K), lambda n, b, t: (0, 0))] * 2
    if pro == "sum_gate":
        args.append(extra)
        specs.append(pl.BlockSpec((1, tm, K), xmap))
    args.append(w)
    if stacked:
        specs.append(pl.BlockSpec((None, tn, K), lambda n, b, t: (n, 0, 0)))
    else:
        specs.append(pl.BlockSpec((K, tn), lambda n, b, t: (0, n)))
    if epi == "glu":
        args.append(w)
        specs.append(pl.BlockSpec((K, tn), lambda n, b, t: (0, nn + n)))
    if bias is not None:
        args.append(bias)
        specs.append(pl.BlockSpec((1, tn), lambda n, b, t: (0, n)))
        if epi == "glu":
            args.append(bias)
            specs.append(pl.BlockSpec((1, tn), lambda n, b, t: (0, nn + n)))
    if epi == "postnorm":
        assert tn == n_out
        h, gate, g, b = post
        args += [h, gate, g, b]
        specs += [pl.BlockSpec((1, tm, n_out), xmap), pl.BlockSpec((1, 1, n_out), vecmap),
                  pl.BlockSpec((1, n_out), lambda n, b_, t: (0, 0)), pl.BlockSpec((1, n_out), lambda n, b_, t: (0, 0))]
    kern = functools.partial(_mm_kernel, pro=pro, epi=epi, has_bias=bias is not None, tm=tm, transposed=stacked)
    if stacked:
        out_spec = pl.BlockSpec((None, 1, tn, tm), lambda n, b, t: (n, b, 0, t))
        out_shape = jax.ShapeDtypeStruct((nn, B, n_out, T), out_dtype)
    else:
        out_spec = pl.BlockSpec((1, tm, tn), lambda n, b, t: (b, t, n))
        out_shape = jax.ShapeDtypeStruct((B, T, n_out), out_dtype)
    return pl.pallas_call(
        kern, grid=grid, in_specs=specs,
        out_specs=out_spec, out_shape=out_shape,
        compiler_params=_cparams(("arbitrary", "arbitrary", "arbitrary")),
        name=name,
    )(*args)


CONV_PAD = 16
CONV_ROWS = 64


def _dwconv_kernel(x_ref, w_ref, b_ref, o_ref, pad_ref, *, T):
    half = CONV_WIDTH // 2
    zeros = jnp.zeros((CONV_PAD, LANES), _f32)
    pad_ref[pl.ds(0, CONV_PAD), :] = zeros
    pad_ref[pl.ds(CONV_PAD + T, CONV_PAD), :] = zeros
    pad_ref[pl.ds(CONV_PAD, T), :] = x_ref[0]
    w = w_ref[...]
    bias = b_ref[...]

    def chunk(i, carry):
        r0 = pl.multiple_of(i * CONV_ROWS, CONV_ROWS)
        win = pad_ref.at[pl.ds(r0, CONV_ROWS + 2 * CONV_PAD), :]
        acc = jnp.zeros((CONV_ROWS, LANES), _f32) + bias
        for j in range(CONV_WIDTH):
            acc = acc + win[pl.ds(CONV_PAD - half + j, CONV_ROWS), :] * w[j:j + 1, :]
        o_ref[0, pl.ds(r0, CONV_ROWS), :] = acc
        return carry

    lax.fori_loop(0, T // CONV_ROWS, chunk, 0)


def _dwconv(u, w_dw, b_dw):
    B, T, C = u.shape
    return pl.pallas_call(
        functools.partial(_dwconv_kernel, T=T),
        grid=(B, C // LANES),
        in_specs=[pl.BlockSpec((1, T, LANES), lambda b, c: (b, 0, c)),
                  pl.BlockSpec((CONV_WIDTH, LANES), lambda b, c: (0, c)),
                  pl.BlockSpec((1, LANES), lambda b, c: (0, c))],
        out_specs=pl.BlockSpec((1, T, LANES), lambda b, c: (b, 0, c)),
        out_shape=jax.ShapeDtypeStruct((B, T, C), _f32),
        scratch_shapes=[pltpu.VMEM((T + 2 * CONV_PAD, LANES), _f32)],
        compiler_params=_cparams(("arbitrary", "arbitrary")),
        name="dwconv",
    )(u, w_dw, b_dw)


NA_Q_ROWS = 4
NA_BAND_ROWS = NA_Q_ROWS + NA_WIN_ROWS


def _na_bias_tables(rpb, rows):
    H = rpb.shape[0]
    W = GRID_W
    padded = jnp.pad(rpb, ((0, 0), (0, 0), (W, W)))
    toep = jnp.stack([padded[:, :, W + NA_WIN_COLS - 1 - qc: 2 * W + NA_WIN_COLS - 1 - qc] for qc in range(W)], axis=2)
    out = []
    for r_s in (0, NA_Q_ROWS, rows - NA_Q_ROWS):
        start = int(np.clip(r_s - NA_WIN_ROWS // 2, 0, rows - NA_BAND_ROWS))
        q_row = r_s + np.repeat(np.arange(NA_Q_ROWS), W)
        q_col = np.tile(np.arange(W), NA_Q_ROWS)
        k_row = start + np.repeat(np.arange(NA_BAND_ROWS), W)
        k_col = np.tile(np.arange(W), NA_BAND_ROWS)
        row0 = np.clip(q_row - NA_WIN_ROWS // 2, 0, rows - NA_WIN_ROWS)
        col0 = np.clip(q_col - NA_WIN_COLS // 2, 0, W - NA_WIN_COLS)
        ok = ((k_row[None, :] >= row0[:, None]) & (k_row[None, :] < row0[:, None] + NA_WIN_ROWS)
              & (k_col[None, :] >= col0[:, None]) & (k_col[None, :] < col0[:, None] + NA_WIN_COLS))
        dr = (start + np.arange(NA_BAND_ROWS))[None, :] - (r_s + np.arange(NA_Q_ROWS))[:, None]
        ridx = np.clip(dr + NA_WIN_ROWS - 1, 0, 2 * NA_WIN_ROWS - 2)
        slabs = jnp.stack([jnp.stack([toep[:, ridx[a, b]] for b in range(NA_BAND_ROWS)], axis=2)
                           for a in range(NA_Q_ROWS)], axis=1)
        bias = slabs.reshape(H, NA_Q_ROWS * W, NA_BAND_ROWS * W)
        out.append(jnp.where(jnp.asarray(ok)[None], bias, NEG_BIG))
    return jnp.stack(out)


def _na_kernel(q_ref, k_ref, v_ref, kc_ref, vc_ref, bias_ref, o_ref, *, rows, dh):
    rt = pl.program_id(2)
    n_rt = pl.num_programs(2)
    nq = NA_Q_ROWS * GRID_W
    nk = NA_BAND_ROWS * GRID_W
    start = jnp.clip(rt * NA_Q_ROWS - NA_WIN_ROWS // 2, 0, rows - NA_BAND_ROWS) * GRID_W
    start = pl.multiple_of(start, GRID_W)
    variant = jnp.where(rt == 0, 0, jnp.where(rt == n_rt - 1, 2, 1))
    q = q_ref[0] * (dh ** -0.5)
    k = k_ref[0, pl.ds(start, nk), :].astype(_bf16)
    v = v_ref[0, pl.ds(start, nk), :].astype(_bf16)
    kc = kc_ref[0].astype(_bf16)
    vc = vc_ref[0].astype(_bf16)
    lane = lax.broadcasted_iota(jnp.int32, (nq, LANES), 1)
    out = jnp.zeros((nq, LANES), _f32)
    for h in range(LANES // dh):
        in_head = (lane >= h * dh) & (lane < (h + 1) * dh)
        qh = jnp.where(in_head, q, 0.0).astype(_bf16)
        s_loc = _dot_nt(qh, k) + bias_ref[variant, 0, h]
        s_ctx = _dot_nt(qh, kc)
        m = jnp.maximum(jnp.max(s_loc, axis=-1, keepdims=True), jnp.max(s_ctx, axis=-1, keepdims=True))
        p_loc = jnp.exp(s_loc - m)
        p_ctx = jnp.exp(s_ctx - m)
        denom = jnp.sum(p_loc, axis=-1, keepdims=True) + jnp.sum(p_ctx, axis=-1, keepdims=True)
        o = (_dot(p_loc.astype(_bf16), v) + _dot(p_ctx.astype(_bf16), vc)) / denom
        out = jnp.where(in_head, o, out)
    o_ref[0] = out.astype(o_ref.dtype)


def _na_attention(qkv, qkv_c, bias_tab):
    B, T, D3 = qkv.shape
    D = D3 // 3
    L = qkv_c.shape[1]
    rows = T // GRID_W
    dh = D // NA_HEADS
    ncb = D // LANES
    nq = NA_Q_ROWS * GRID_W
    nk = NA_BAND_ROWS * GRID_W
    per = LANES // dh
    return pl.pallas_call(
        functools.partial(_na_kernel, rows=rows, dh=dh),
        grid=(ncb, B, rows // NA_Q_ROWS),
        in_specs=[pl.BlockSpec((1, nq, LANES), lambda c, b, r: (b, r, c)),
                  pl.BlockSpec((1, T, LANES), lambda c, b, r: (b, 0, ncb + c)),
                  pl.BlockSpec((1, T, LANES), lambda c, b, r: (b, 0, 2 * ncb + c)),
                  pl.BlockSpec((1, L, LANES), lambda c, b, r: (b, 0, ncb + c)),
                  pl.BlockSpec((1, L, LANES), lambda c, b, r: (b, 0, 2 * ncb + c)),
                  pl.BlockSpec((3, 1, per, nq, nk), lambda c, b, r: (0, c, 0, 0, 0))],
        out_specs=pl.BlockSpec((1, nq, LANES), lambda c, b, r: (b, r, c)),
        out_shape=jax.ShapeDtypeStruct((B, T, D), _bf16),
        compiler_params=_cparams(("arbitrary", "arbitrary", "arbitrary")),
        name="na_attn",
    )(qkv, qkv, qkv, qkv_c, qkv_c, bias_tab)


def _ctx_attn_kernel(q_ref, k_ref, v_ref, o_ref, *, dh):
    L = q_ref.shape[1]
    q = q_ref[0] * (dh ** -0.5)
    k = k_ref[0].astype(_bf16)
    v = v_ref[0].astype(_bf16)
    lane = lax.broadcasted_iota(jnp.int32, (L, LANES), 1)
    out = jnp.zeros((L, LANES), _f32)
    for h in range(LANES // dh):
        in_head = (lane >= h * dh) & (lane < (h + 1) * dh)
        qh = jnp.where(in_head, q, 0.0).astype(_bf16)
        s = _dot_nt(qh, k)
        p = jnp.exp(s - jnp.max(s, axis=-1, keepdims=True))
        o = _dot(p.astype(_bf16), v) / jnp.sum(p, axis=-1, keepdims=True)
        out = jnp.where(in_head, o, out)
    o_ref[0] = out.astype(o_ref.dtype)


def _ctx_attention(qkv_c):
    B, L, D3 = qkv_c.shape
    D = D3 // 3
    ncb = D // LANES
    dh = D // NA_HEADS
    return pl.pallas_call(
        functools.partial(_ctx_attn_kernel, dh=dh),
        grid=(B, ncb),
        in_specs=[pl.BlockSpec((1, L, LANES), lambda b, c: (b, 0, c)),
                  pl.BlockSpec((1, L, LANES), lambda b, c: (b, 0, ncb + c)),
                  pl.BlockSpec((1, L, LANES), lambda b, c: (b, 0, 2 * ncb + c))],
        out_specs=pl.BlockSpec((1, L, LANES), lambda b, c: (b, 0, c)),
        out_shape=jax.ShapeDtypeStruct((B, L, D), _bf16),
        compiler_params=_cparams(("arbitrary", "arbitrary")),
        name="ctx_attn",
    )(qkv_c, qkv_c, qkv_c)


def _rw_lora_kernel(x_ref, sh_ref, sc_ref, hp_ref, hn_ref, mup_ref, mun_ref, w1_ref, a1_ref, g1_ref,
                    w2_ref, a2_ref, g2_ref, w0_ref, a0_ref, lw_ref, ia_ref, *maybe_g, tm, emit):
    sh, sc1 = sh_ref[0], 1.0 + sc_ref[0]
    x = x_ref[0] * sc1 + sh
    t = pl.program_id(1)
    nt = pl.num_programs(1)
    prev_row = jnp.where(t == 0, 0.0, hp_ref[0, 0] * sc1 + sh)
    next_row = jnp.where(t == nt - 1, 0.0, hn_ref[0, 0] * sc1 + sh)
    rows = lax.broadcasted_iota(jnp.int32, x.shape, 0)
    d_prev = jnp.where(rows == 0, prev_row, pltpu.roll(x, 1, 0)) - x
    d_next = jnp.where(rows == tm - 1, next_row, pltpu.roll(x, tm - 1, 0)) - x

    def lerp(n):
        return (x + d_prev * mup_ref[n] + d_next * mun_ref[n]).astype(_bf16)

    tw = jnp.tanh(_dot(lerp(0), w1_ref[...])).astype(_bf16)
    ta = _dot(lerp(1), a1_ref[...]).astype(_bf16)
    for d in range(2):
        z = w0_ref[d] + _dot(tw, w2_ref[d])
        softplus_neg = jnp.maximum(-z, 0.0) + jnp.log(1.0 + jnp.exp(-jnp.abs(z)))
        lw_ref[d, 0] = -jnp.exp(-softplus_neg - 0.5)
        ia_ref[d, 0] = _sigmoid(a0_ref[d] + _dot(ta, a2_ref[d]))
    if emit:
        g_ref = maybe_g[0]
        tg = _sigmoid(_dot(lerp(2), g1_ref[...])).astype(_bf16)
        g_ref[0] = _dot(tg, g2_ref[...])


def _rw_lora(h, mod, mu_prev3, mu_next3, w1c, a1c, g1, w2p, a2p, g2, w0, a0, emit):
    B, T, D = h.shape
    tm = _row_tile(T)
    nt = T // tm
    shift, scale = mod
    zero = jnp.zeros((B, 1, D), h.dtype)
    hp = jnp.concatenate([zero, h[:, tm - 1:T - 1:tm]], axis=1)[:, :, None]
    hn = jnp.concatenate([h[:, tm::tm], zero], axis=1)[:, :, None]
    R2 = w1c.shape[1]
    G = g1.shape[1]
    c2 = lambda b, t: (0, 0)
    c3 = lambda b, t: (0, 0, 0)
    out_shape = [jax.ShapeDtypeStruct((2, B, T, D), _f32), jax.ShapeDtypeStruct((2, B, T, D), _f32)]
    out_specs = [pl.BlockSpec((2, 1, tm, D), lambda b, t: (0, b, t, 0))] * 2
    if emit:
        out_shape.append(jax.ShapeDtypeStruct((B, T, D), _f32))
        out_specs.append(pl.BlockSpec((1, tm, D), lambda b, t: (b, t, 0)))
    return pl.pallas_call(
        functools.partial(_rw_lora_kernel, tm=tm, emit=emit),
        grid=(B, nt),
        in_specs=[pl.BlockSpec((1, tm, D), lambda b, t: (b, t, 0)),
                  pl.BlockSpec((1, 1, D), lambda b, t: (b, 0, 0)),
                  pl.BlockSpec((1, 1, D), lambda b, t: (b, 0, 0)),
                  pl.BlockSpec((1, 1, 1, D), lambda b, t: (b, t, 0, 0)),
                  pl.BlockSpec((1, 1, 1, D), lambda b, t: (b, t, 0, 0)),
                  pl.BlockSpec((3, 1, D), c3), pl.BlockSpec((3, 1, D), c3),
                  pl.BlockSpec((D, R2), c2), pl.BlockSpec((D, R2), c2), pl.BlockSpec((D, G), c2),
                  pl.BlockSpec((2, R2, D), c3), pl.BlockSpec((2, R2, D), c3), pl.BlockSpec((G, D), c2),
                  pl.BlockSpec((2, 1, D), c3), pl.BlockSpec((2, 1, D), c3)],
        out_specs=out_specs, out_shape=out_shape,
        compiler_params=_cparams(("arbitrary", "arbitrary")),
        name="rw_lora",
    )(h, shift, scale, hp, hn, mu_prev3, mu_next3, w1c, a1c, g1, w2p, a2p, g2, w0, a0)


RW_TC = 32


def _rw_scan_kernel(*refs, emit, has_init, dh):
    it = iter(refs)
    lw_ref, ia_ref, k_ref, v_ref = next(it), next(it), next(it), next(it)
    r_ref = next(it) if emit else None
    kk_ref, ka_ref = next(it), next(it)
    if emit:
        rk_ref, gg_ref, gb_ref = next(it), next(it), next(it)
    s0_ref = next(it) if has_init else None
    y_ref = next(it) if emit else None
    sfin_ref = next(it)
    s_ref, w_s, a_s, b_s, kd_s = next(it), next(it), next(it), next(it), next(it)

    d = pl.program_id(0)
    c = pl.program_id(2)
    nc = pl.num_programs(2)

    @pl.when(c == 0)
    def _():
        if has_init:
            s_ref[...] = s0_ref[0, 0]
        else:
            s_ref[...] = jnp.zeros(s_ref.shape, _f32)

    def step(i, carry):
        t = jnp.where(d == 0, i, RW_TC - 1 - i)
        k_t = k_ref[t]
        ia_t = ia_ref[0, t]
        v_t = v_ref[t]
        kkr = k_t * kk_ref[...]
        kk = kkr * lax.rsqrt(jnp.maximum(jnp.sum(kkr * kkr, axis=0, keepdims=True), 1e-24))
        kd = k_t * (1.0 + (ia_t - 1.0) * ka_ref[...])
        w_s[...] = jnp.exp(lw_ref[0, t])
        a_s[...] = -kk
        b_s[...] = kk * ia_t
        kd_s[...] = kd
        part = [jnp.zeros((dh, LANES), _f32), jnp.zeros((dh, LANES), _f32)]
        for kj in range(dh):
            part[kj % 2] = part[kj % 2] + s_ref[kj] * a_s[kj:kj + 1, :]
        sa = part[0] + part[1]
        part = [jnp.zeros((dh, LANES), _f32), jnp.zeros((dh, LANES), _f32)]
        for kj in range(dh):
            sk = s_ref[kj] * w_s[kj:kj + 1, :] + sa * b_s[kj:kj + 1, :] + v_t * kd_s[kj:kj + 1, :]
            s_ref[kj] = sk
            if emit:
                part[kj % 2] = part[kj % 2] + sk * r_ref[t, pl.ds(kj, 1), :]
        if emit:
            r_t = r_ref[t]
            y = part[0] + part[1]
            mu = jnp.mean(y, axis=0, keepdims=True)
            yc = y - mu
            var = jnp.mean(yc * yc, axis=0, keepdims=True)
            bonus = jnp.sum(r_t * kd * rk_ref[...], axis=0, keepdims=True) * v_t
            y_ref[0, t] = yc * lax.rsqrt(var + GN_EPS) * gg_ref[...] + gb_ref[...] + bonus
        return carry

    lax.fori_loop(0, RW_TC, step, 0, unroll=2)

    @pl.when(c == nc - 1)
    def _():
        sfin_ref[0, 0] = s_ref[...]


def _rw_scan(lwT, iaT, kvrT, par, s0, emit):
    _, T, dh, S = lwT.shape
    nsb = S // LANES
    nc = T // RW_TC

    def tmap(d, s, c):
        return (jnp.where(d == 0, c, nc - 1 - c), 0, s)

    def dtmap(d, s, c):
        return (d, jnp.where(d == 0, c, nc - 1 - c), 0, s)

    def tile(g):
        return pl.BlockSpec((None, RW_TC, dh, LANES), lambda d, s, c: (g,) + tmap(d, s, c))

    dtile = pl.BlockSpec((1, RW_TC, dh, LANES), dtmap)
    ptile = pl.BlockSpec((dh, LANES), lambda d, s, c: (0, s))
    stile = pl.BlockSpec((1, 1, dh, dh, LANES), lambda d, s, c: (d, s, 0, 0, 0))
    args, specs = [lwT, iaT, kvrT, kvrT], [dtile, dtile, tile(0), tile(1)]
    if emit:
        args.append(kvrT)
        specs.append(tile(2))
    args += [par["k_k"], par["k_a"]]
    specs += [ptile, ptile]
    if emit:
        args += [par["r_k"], par["gn_g"], par["gn_b"]]
        specs += [ptile] * 3
    if s0 is not None:
        args.append(s0)
        specs.append(stile)
    out_shape, out_specs = [], []
    if emit:
        out_shape.append(jax.ShapeDtypeStruct((2, T, dh, S), _f32))
        out_specs.append(dtile)
    out_shape.append(jax.ShapeDtypeStruct((2, nsb, dh, dh, LANES), _f32))
    out_specs.append(stile)
    res = pl.pallas_call(
        functools.partial(_rw_scan_kernel, emit=emit, has_init=s0 is not None, dh=dh),
        grid=(2, nsb, nc),
        in_specs=specs, out_specs=out_specs, out_shape=out_shape,
        scratch_shapes=[pltpu.VMEM((dh, dh, LANES), _f32)] + [pltpu.VMEM((dh, LANES), _f32)] * 4,
        compiler_params=_cparams(("arbitrary", "arbitrary", "arbitrary")),
        name="rw_scan_emit" if emit else "rw_scan_state",
    )(*args)
    return (res[0], res[1]) if emit else (None, res[0])


def _excl_cumsum_lanes(m):
    E, T = m.shape
    assert E & (E - 1) == 0
    nb = T // LANES
    R = nb * E
    r = lax.broadcasted_iota(jnp.int32, (LANES, LANES), 0)
    c = lax.broadcasted_iota(jnp.int32, (LANES, LANES), 1)
    tri = jnp.where(r < c, 1.0, 0.0).astype(_bf16)
    ones = jnp.ones((LANES, LANES), _bf16)
    stacked = jnp.concatenate([m[:, j * LANES:(j + 1) * LANES] for j in range(nb)], axis=0).astype(_bf16)
    within = _dot(stacked, tri)
    totals = _dot(stacked, ones)
    qi = lax.broadcasted_iota(jnp.int32, (R, R), 0)
    qj = lax.broadcasted_iota(jnp.int32, (R, R), 1)
    shift = E.bit_length() - 1
    earlier = ((qi & (E - 1)) == (qj & (E - 1))) & (lax.shift_right_logical(qj, shift) < lax.shift_right_logical(qi, shift))
    res = within + _dot(jnp.where(earlier, 1.0, 0.0).astype(_bf16), totals.astype(_bf16))
    return jnp.concatenate([res[j * E:(j + 1) * E, :] for j in range(nb)], axis=1)


def _route_kernel(x_ref, sh_ref, sc_ref, wr_ref, xm_ref, slot_ref, gate_ref, idx_ref, lt_ref, *, tm, cap, n_exp):
    t = pl.program_id(1)
    nt = pl.num_programs(1)
    xm = x_ref[0] * (1.0 + sc_ref[0]) + sh_ref[0]
    bits = pltpu.bitcast(xm, jnp.int32)
    rnd = lax.shift_right_logical(bits + 0x7FFF + (lax.shift_right_logical(bits, 16) & 1), 16)
    half = xm.shape[1] // 2
    xm_ref[0] = lax.shift_left(rnd[:, :half], 16) | rnd[:, half:]
    logits = jnp.dot(xm, wr_ref[...], preferred_element_type=_f32, precision=lax.Precision.HIGHEST)
    lt_ref[:, pl.ds(pl.multiple_of(t * tm, tm), tm)] = logits.T

    @pl.when(t == nt - 1)
    def _():
        lg = lt_ref[0:n_exp, :]
        e = jnp.exp(lg - jnp.max(lg, axis=0, keepdims=True))
        aff = e / jnp.sum(e, axis=0, keepdims=True)
        bits = pltpu.bitcast(aff, jnp.int32)
        thr = jnp.zeros((n_exp, 1), jnp.int32)
        for bit in range(30, -1, -1):
            cand = thr | (1 << bit)
            cnt = jnp.sum(jnp.where(bits >= cand, 1.0, 0.0), axis=1, keepdims=True)
            thr = jnp.where(cnt >= cap, cand, thr)
        gt = jnp.where(bits > thr, 1.0, 0.0)
        eq = jnp.where(bits == thr, 1.0, 0.0)
        need = cap - jnp.sum(gt, axis=1, keepdims=True)
        sel = gt + eq * jnp.where(_excl_cumsum_lanes(eq) < need, 1.0, 0.0)
        pos = _excl_cumsum_lanes(sel)
        slot = jnp.where(sel > 0.0, pos, -1.0).astype(jnp.int32)
        slot_ref[0] = slot
        gate_ref[0] = sel * aff
        T = slot.shape[1]
        srow = lax.broadcasted_iota(jnp.int32, (cap, T), 0).astype(_f32)
        cinc = pos + sel
        for e in range(n_exp):
            before = jnp.where(cinc[e:e + 1, :] <= srow, 1.0, 0.0)
            idx_ref[0, e] = jnp.sum(before, axis=1, keepdims=True).astype(jnp.int32)


def _moe_route(h, mod, w_router_pad, cap):
    B, T, D = h.shape
    tm = _row_tile(T)
    shift, scale = mod
    E = N_EXPERTS
    return pl.pallas_call(
        functools.partial(_route_kernel, tm=tm, cap=cap, n_exp=E),
        grid=(B, T // tm),
        in_specs=[pl.BlockSpec((1, tm, D), lambda b, t: (b, t, 0)),
                  pl.BlockSpec((1, 1, D), lambda b, t: (b, 0, 0)),
                  pl.BlockSpec((1, 1, D), lambda b, t: (b, 0, 0)),
                  pl.BlockSpec((D, LANES), lambda b, t: (0, 0))],
        out_specs=[pl.BlockSpec((1, tm, D // 2), lambda b, t: (b, t, 0)),
                   pl.BlockSpec((1, E, T), lambda b, t: (b, 0, 0)),
                   pl.BlockSpec((1, E, T), lambda b, t: (b, 0, 0)),
                   pl.BlockSpec((1, E, cap, 1), lambda b, t: (b, 0, 0, 0))],
        out_shape=[jax.ShapeDtypeStruct((B, T, D // 2), jnp.int32),
                   jax.ShapeDtypeStruct((B, E, T), jnp.int32),
                   jax.ShapeDtypeStruct((B, E, T), _f32),
                   jax.ShapeDtypeStruct((B, E, cap, 1), jnp.int32)],
        scratch_shapes=[pltpu.VMEM((LANES, T), _f32)],
        compiler_params=_cparams(("arbitrary", "arbitrary")),
        name="moe_route",
    )(h, shift, scale, w_router_pad)


FFN_CHUNKS = 4


def _expert_kernel(idx_ref, xm_ref, slot_ref, gate_ref, wg_ref, wu_ref, wd_ref, ys_ref, xs_ref, *, bb, cap):
    T = xm_ref.shape[1]
    srow = lax.broadcasted_iota(jnp.int32, (cap, T), 0)
    gates = []
    for i in range(bb):
        for s in range(cap):
            xs_ref[pl.ds(i * cap + s, 1), :] = xm_ref[i, pl.ds(idx_ref[i, 0, 0, s], 1), :]
        hit = slot_ref[i, 0] == srow
        gates.append(jnp.sum(jnp.where(hit, gate_ref[i, 0], 0.0), axis=1, keepdims=True))
    packed = xs_ref[...]
    xs = jnp.concatenate([pltpu.bitcast(packed & jnp.int32(-65536), _f32),
                          pltpu.bitcast(lax.shift_left(packed, 16), _f32)], axis=1).astype(_bf16)
    F = wg_ref.shape[2]
    fc = F // FFN_CHUNKS
    y = None
    for c in range(FFN_CHUNKS):
        cols = pl.ds(c * fc, fc)
        hid = (_silu(_dot(xs, wg_ref[0, :, cols])) * _dot(xs, wu_ref[0, :, cols])).astype(_bf16)
        part = _dot(hid, wd_ref[0, cols, :])
        y = part if y is None else y + part
    for i in range(bb):
        ys_ref[i, 0] = (y[i * cap:(i + 1) * cap] * gates[i]).astype(_bf16)


def _moe_experts(xm, slot, gate, idx, wg, wu, wd, layer, cap, bb):
    B, T, half = xm.shape
    _, E, D, F = wg.shape
    slot4 = slot[:, :, None, :]
    gate4 = gate[:, :, None, :]
    idx4 = idx.reshape(B, E, 1, cap)
    once = pl.Buffered(1)
    return pl.pallas_call(
        functools.partial(_expert_kernel, bb=bb, cap=cap),
        grid=(E, B // bb),
        in_specs=[pl.BlockSpec((bb, 1, 1, cap), lambda e, g: (g, e, 0, 0), memory_space=pltpu.SMEM),
                  pl.BlockSpec((bb, T, half), lambda e, g: (g, 0, 0)),
                  pl.BlockSpec((bb, 1, 1, T), lambda e, g: (g, e, 0, 0)),
                  pl.BlockSpec((bb, 1, 1, T), lambda e, g: (g, e, 0, 0)),
                  pl.BlockSpec((None, 1, D, F), lambda e, g: (layer, e, 0, 0), pipeline_mode=once),
                  pl.BlockSpec((None, 1, D, F), lambda e, g: (layer, e, 0, 0), pipeline_mode=once),
                  pl.BlockSpec((None, 1, F, D), lambda e, g: (layer, e, 0, 0), pipeline_mode=once)],
        out_specs=pl.BlockSpec((bb, 1, cap, D), lambda e, g: (g, e, 0, 0)),
        out_shape=jax.ShapeDtypeStruct((B, E, cap, D), _bf16),
        scratch_shapes=[pltpu.VMEM((bb * cap, half), jnp.int32)],
        compiler_params=_cparams(("arbitrary", "arbitrary")),
        name="moe_experts",
    )(idx4, xm, slot4, gate4, wg, wu, wd)


COMBINE_GROUP = 4


def _combine_kernel(slotc_ref, ys_ref, h_ref, ga_ref, g_ref, b_ref, o_ref, *, tm, cap, n_exp):
    col = lax.broadcasted_iota(jnp.int32, (tm, cap), 1)
    sc = slotc_ref[0]
    y = None
    for e0 in range(0, n_exp, COMBINE_GROUP):
        hot = [jnp.where(sc[:, e:e + 1] == col, 1.0, 0.0).astype(_bf16) for e in range(e0, e0 + COMBINE_GROUP)]
        part = _dot(jnp.concatenate(hot, axis=1), ys_ref[0, pl.ds(e0 * cap, COMBINE_GROUP * cap), :])
        y = part if y is None else y + part
    o_ref[0] = _ln_rows(DN_ALPHA * h_ref[0] + ga_ref[0] * y, g_ref[...], b_ref[...], LN_EPS)


def _moe_combine(slot_col, ys, h, gate_vec, g, b, cap):
    B, T, D = h.shape
    E = N_EXPERTS
    tm = _row_tile(T)
    return pl.pallas_call(
        functools.partial(_combine_kernel, tm=tm, cap=cap, n_exp=E),
        grid=(B, T // tm),
        in_specs=[pl.BlockSpec((1, tm, E), lambda b_, t: (b_, t, 0)),
                  pl.BlockSpec((1, E * cap, D), lambda b_, t: (b_, 0, 0)),
                  pl.BlockSpec((1, tm, D), lambda b_, t: (b_, t, 0)),
                  pl.BlockSpec((1, 1, D), lambda b_, t: (b_, 0, 0)),
                  pl.BlockSpec((1, D), lambda b_, t: (0, 0)),
                  pl.BlockSpec((1, D), lambda b_, t: (0, 0))],
        out_specs=pl.BlockSpec((1, tm, D), lambda b_, t: (b_, t, 0)),
        out_shape=jax.ShapeDtypeStruct((B, T, D), _f32),
        compiler_params=_cparams(("arbitrary", "arbitrary")),
        name="moe_combine",
    )(slot_col, ys, h, gate_vec, g, b)


def _moe_block(h, mod2, gate_vec, ln_g, ln_b, w_router_pad, wg, wu, wd, layer):
    B, T, D = h.shape
    cap = max(1, EC_CAPACITY * T // N_EXPERTS)
    xm, slot, gate, idx = _moe_route(h, mod2, w_router_pad, cap)
    bb = max(1, min(B, 512 // cap))
    ys = _moe_experts(xm, slot, gate, idx, wg, wu, wd, layer, cap, bb)
    slot_col = jnp.swapaxes(slot, 1, 2)
    return _moe_combine(slot_col, ys.reshape(B, N_EXPERTS * cap, D), h, gate_vec, ln_g, ln_b, cap)


def _to_seq_major(t, dh):
    *lead, B, T, D = t.shape
    n = len(lead)
    perm = tuple(range(n)) + (n + 1, n + 3, n, n + 2)
    return jnp.transpose(t.reshape(*lead, B, T, D // dh, dh), perm).reshape(*lead, T, dh, B * (D // dh))


def _from_seq_major(t, B):
    *lead, T, dh, S = t.shape
    n = len(lead)
    perm = tuple(range(n)) + (n + 2, n, n + 3, n + 1)
    return jnp.transpose(t.reshape(*lead, T, dh, B, S // B), perm).reshape(*lead, B, T, (S // B) * dh)


def _head_param(p, B, dh):
    return jnp.tile(p.reshape(-1, dh).T, (1, B))


def _rwkv_stream(h, mod, p, s0, emit):
    B, T, D = h.shape
    dh = RWKV_HEAD_DIM
    outs = _rw_lora(h, mod, p["mu_prev3"], p["mu_next3"], p["w1c"], p["a1c"], p["g1"], p["w2p"], p["a2p"], p["g2"],
                    p["w0"], p["a0"], emit)
    lw, ia = outs[0], outs[1]
    n_proj = 3 if emit else 2
    kvr = _mm(h, p["w_kvr"][:n_proj], pro="mod_lerp", mod=mod,
              lerp=(p["mu_prev_kvr"][:n_proj], p["mu_next_kvr"][:n_proj]), name="rw_kvr")
    par = {n: _head_param(p[n], B, dh) for n in ("k_k", "k_a", "r_k", "gn_g", "gn_b")}
    G = kvr.shape[0]
    kvrT = jnp.transpose(kvr.reshape(G, B, D // dh, dh, T), (0, 4, 3, 1, 2)).reshape(G, T, dh, B * (D // dh))
    yT, states = _rw_scan(_to_seq_major(lw, dh), _to_seq_major(ia, dh), kvrT, par, s0, emit)
    if not emit:
        return None, states
    return (_from_seq_major(yT, B), outs[2]), states


def kernel(x, c, ctx, c_ctx, ada_w, ada_b, ln_g, ln_b, conv_w_in, conv_b_in, conv_w_dw, conv_b_dw, conv_ln_g, conv_ln_b, conv_w_out, conv_b_out, na_w_qkv, na_w_o, na_rpb, rw_mu_prev, rw_mu_next, rw_w_r, rw_w_k, rw_w_v, rw_w0, rw_w1, rw_w2, rw_a0, rw_a1, rw_a2, rw_k_k, rw_k_a, rw_r_k, rw_g1, rw_g2, rw_gn_g, rw_gn_b, rw_w_o, moe_router, moe_w_gate, moe_w_up, moe_w_down):
    B, T, D = x.shape
    L = ctx.shape[1]
    depth = ada_w.shape[0]
    E = moe_router.shape[-1]
    bf = lambda a: a.astype(_bf16)
    row = lambda a: a.reshape(1, -1)

    ctx_layers = [i for i in range(depth) if i % N_MIXERS != 0]
    last_ctx = max(ctx_layers) if ctx_layers else -1

    pad_rows = (-(B + 1)) % SUBLANES
    cond = jnp.concatenate([c, c_ctx[None, :], jnp.zeros((pad_rows, D), c.dtype)], axis=0)
    ada = _ada_all(cond, bf(ada_w), ada_b[:, None, :])

    def mods(i, is_ctx):
        m = ada[i, B:B + 1] if is_ctx else ada[i, :B]
        m = jnp.broadcast_to(m[:, None, :], (B, 1, 6 * D))
        return [m[:, :, k * D:(k + 1) * D] for k in range(6)]

    wg, wu, wd = bf(moe_w_gate), bf(moe_w_up), bf(moe_w_down)
    h, hc = x, ctx
    for i in range(depth):
        kind, j = i % N_MIXERS, i // N_MIXERS
        ctx_read = i <= last_ctx
        ctx_live = i < last_ctx
        sh1, sc1, ga1, sh2, sc2, ga2 = mods(i, False)
        if ctx_read:
            csh1, csc1, cga1, csh2, csc2, cga2 = mods(i, True)
        g1n, b1n = row(ln_g[i, 0]), row(ln_b[i, 0])
        g2n, b2n = row(ln_g[i, 1]), row(ln_b[i, 1])

        if kind == 0:
            w_in, w_out = bf(conv_w_in[j]), bf(conv_w_out[j])

            def conv_stream(hh, shift, scale, gate):
                u = _mm(hh, w_in, pro="mod", mod=(shift, scale), epi="glu", bias=row(conv_b_in[j]), name="conv_in")
                u = _dwconv(u, conv_w_dw[j], row(conv_b_dw[j]))
                return _mm(u, w_out, pro="ln_silu", ln=(row(conv_ln_g[j]), row(conv_ln_b[j])), bias=row(conv_b_out[j]),
                           epi="postnorm", post=(hh, gate, g1n, b1n), name="conv_out")

            h_new = conv_stream(h, sh1, sc1, ga1)
            hc_new = conv_stream(hc, csh1, csc1, cga1) if ctx_live else None
        elif kind == 1:
            w_qkv, w_o = bf(na_w_qkv[j]), bf(na_w_o[j])
            qkv = _mm(h, w_qkv, pro="mod", mod=(sh1, sc1), tn=D, out_dtype=_bf16, name="na_qkv")
            qkv_c = _mm(hc, w_qkv, pro="mod", mod=(csh1, csc1), tn=D, out_dtype=_bf16, name="na_qkv_ctx")
            tab = _na_bias_tables(na_rpb[j], T // GRID_W)
            per = LANES // (D // NA_HEADS)
            tab = tab.reshape(3, NA_HEADS // per, per, tab.shape[2], tab.shape[3])
            att = _na_attention(qkv, qkv_c, tab)
            h_new = _mm(att, w_o, epi="postnorm", post=(h, ga1, g1n, b1n), name="na_out")
            hc_new = None
            if ctx_live:
                att_c = _ctx_attention(qkv_c)
                hc_new = _mm(att_c, w_o, epi="postnorm", post=(hc, cga1, g1n, b1n), name="na_out_ctx")
        else:
            R = rw_w1.shape[-1]
            zw = jnp.zeros((R, D), _f32)
            p = {
                "mu_prev3": rw_mu_prev[j][jnp.array([1, 4, 5])][:, None, :],
                "mu_next3": rw_mu_next[j][jnp.array([1, 4, 5])][:, None, :],
                "mu_prev_kvr": rw_mu_prev[j][jnp.array([2, 3, 0])][:, None, :],
                "mu_next_kvr": rw_mu_next[j][jnp.array([2, 3, 0])][:, None, :],
                "w_kvr": bf(jnp.stack([rw_w_k[j].T, rw_w_v[j].T, rw_w_r[j].T])),
                "w1c": bf(jnp.concatenate([rw_w1[j, 0], rw_w1[j, 1]], axis=1)),
                "a1c": bf(jnp.concatenate([rw_a1[j, 0], rw_a1[j, 1]], axis=1)),
                "g1": bf(rw_g1[j]), "g2": bf(rw_g2[j]),
                "w2p": bf(jnp.stack([jnp.concatenate([rw_w2[j, 0], zw]), jnp.concatenate([zw, rw_w2[j, 1]])])),
                "a2p": bf(jnp.stack([jnp.concatenate([rw_a2[j, 0], zw]), jnp.concatenate([zw, rw_a2[j, 1]])])),
                "w0": rw_w0[j][:, None, :], "a0": rw_a0[j][:, None, :],
                "k_k": rw_k_k[j], "k_a": rw_k_a[j], "r_k": rw_r_k[j].reshape(-1),
                "gn_g": rw_gn_g[j], "gn_b": rw_gn_b[j],
            }
            reads_c, states = _rwkv_stream(hc, (csh1, csc1), p, None, ctx_live)
            reads, _ = _rwkv_stream(h, (sh1, sc1), p, states, True)
            w_o = bf(rw_w_o[j])
            h_new = _mm(reads[0], w_o, pro="sum_gate", extra=reads[1], epi="postnorm",
                        post=(h, ga1, g1n, b1n), name="rw_out")
            hc_new = None
            if ctx_live:
                hc_new = _mm(reads_c[0], w_o, pro="sum_gate", extra=reads_c[1], epi="postnorm",
                             post=(hc, cga1, g1n, b1n), name="rw_out_ctx")

        wr = jnp.pad(moe_router[i], ((0, 0), (0, LANES - E)))
        h = _moe_block(h_new, (sh2, sc2), ga2, g2n, b2n, wr, wg, wu, wd, i)
        if ctx_live:
            hc = _moe_block(hc_new, (csh2, csc2), cga2, g2n, b2n, wr, wg, wu, wd, i)
    return h
```
